```python
import math, functools
import jax, jax.numpy as jnp
from jax import lax
import numpy as np

D_MODEL = 2048
BATCH = 2
SEQ = 4096
DEPTH = 1
DEC_BATCH = 32
DEC_SEQ = 1
PAST_LEN = 16384
PAGE_SIZE = 128

ATTN_HEADS = 8
QK_DIM = 64
HEAD_QK = 2 * QK_DIM
HEAD_V = 2 * QK_DIM
ATTN_WIDTH = ATTN_HEADS * HEAD_V
CONV_DIM = D_MODEL // 2
CONV_WIDTH = 3
D_FF = -(-8 * D_MODEL // (3 * 256)) * 256
ROPE_THETA = 10000.0
Q_BLOCK = 128
EPS = 1e-5
SCALE = QK_DIM ** -0.5
NEG_INF = -1e30
IN_COLS = 3 * ATTN_WIDTH + 3 * CONV_DIM + 2 * D_MODEL
SPLITS = (ATTN_WIDTH, 2 * ATTN_WIDTH, 3 * ATTN_WIDTH,
          3 * ATTN_WIDTH + CONV_DIM, 3 * ATTN_WIDTH + 2 * CONV_DIM,
          3 * ATTN_WIDTH + 3 * CONV_DIM, 3 * ATTN_WIDTH + 3 * CONV_DIM + D_MODEL)

kernel_name = "diffattn_shortconv_gated_hybrid_step"


def rmsnorm(x, g):
    xf = x.astype(jnp.float32)
    y = xf * lax.rsqrt(jnp.mean(xf * xf, axis=-1, keepdims=True) + EPS)
    return (y * g.astype(jnp.float32)).astype(x.dtype)


def rope(x, pos):
    half = QK_DIM // 2
    inv = ROPE_THETA ** (-jnp.arange(half, dtype=jnp.float32) / half)
    ang = pos.astype(jnp.float32)[:, None] * inv[None, :]
    cos = jnp.cos(ang)[:, None, None, :]
    sin = jnp.sin(ang)[:, None, None, :]
    xf = x.astype(jnp.float32)
    x1, x2 = xf[..., :half], xf[..., half:]
    return jnp.concatenate([x1 * cos - x2 * sin, x2 * cos + x1 * sin], axis=-1).astype(x.dtype)


def diff_lambda(lq1, lk1, lq2, lk2, lam_init):
    f = jnp.float32
    return (jnp.exp(jnp.sum(lq1.astype(f) * lk1.astype(f)))
            - jnp.exp(jnp.sum(lq2.astype(f) * lk2.astype(f))) + lam_init)


def diff_combine(s, lam):
    p = jax.nn.softmax(s, axis=-1)
    return p[:, 0] - lam * p[:, 1]


def attend_prompt(q, k, v, lam):
    B, S = q.shape[0], q.shape[1]
    nb = S // Q_BLOCK
    qb = jnp.moveaxis(q.reshape(B, nb, Q_BLOCK, ATTN_HEADS, 2, QK_DIM), 1, 0)
    starts = jnp.arange(nb, dtype=jnp.int32) * Q_BLOCK
    kpos = jnp.arange(S, dtype=jnp.int32)

    def block(args):
        qblk, start = args
        s = jnp.einsum('bqhcd,bkhcd->bchqk', qblk, k).astype(jnp.float32) * SCALE
        qpos = start + jnp.arange(Q_BLOCK, dtype=jnp.int32)
        s = jnp.where(kpos[None, :] <= qpos[:, None], s, NEG_INF)
        a = diff_combine(s, lam)
        return jnp.einsum('bhqk,bkhe->bqhe', a.astype(v.dtype), v)

    o = lax.map(block, (qb, starts))
    return jnp.moveaxis(o, 0, 1).reshape(B, S, ATTN_HEADS, HEAD_V)


def attend_sample(q, k, v, lam, k_past, v_past):
    Sd = q.shape[1]
    P = k_past.shape[1]
    s_past = jnp.einsum('bqhcd,bkhcd->bchqk', q, k_past).astype(jnp.float32) * SCALE
    s_new = jnp.einsum('bqhcd,bkhcd->bchqk', q, k).astype(jnp.float32) * SCALE
    causal = jnp.tril(jnp.ones((Sd, Sd), dtype=bool))
    s_new = jnp.where(causal, s_new, NEG_INF)
    a = diff_combine(jnp.concatenate([s_past, s_new], axis=-1), lam)
    return (jnp.einsum('bhqk,bkhe->bqhe', a[..., :P].astype(v.dtype), v_past)
            + jnp.einsum('bhqk,bkhe->bqhe', a[..., P:].astype(v.dtype), v))


def causal_short_conv(u_pad, w, S):
    y = w[0] * u_pad[:, 0:S]
    for j in range(1, CONV_WIDTH):
        y = y + w[j] * u_pad[:, j:j + S]
    return y


def decoder_layer(x, pos, conv_prev, attn_fn, layer_idx, w_in, w_conv, lq1, lk1, lq2, lk2,
                  g_subln, w_br_a, w_br_c, w_out, g_mix, g_ffn, w_gu, w_down):
    B, S = x.shape[0], x.shape[1]
    lam_init = 0.8 - 0.6 * math.exp(-0.3 * layer_idx)
    h = rmsnorm(x, g_mix)
    z = h @ w_in
    q, k, v, cb, cc, cx, ga, gc = jnp.split(z, SPLITS, axis=-1)
    q = rope(q.reshape(B, S, ATTN_HEADS, 2, QK_DIM), pos)
    k = rope(k.reshape(B, S, ATTN_HEADS, 2, QK_DIM), pos)
    v = v.reshape(B, S, ATTN_HEADS, HEAD_V)
    lam = diff_lambda(lq1, lk1, lq2, lk2, lam_init)
    o = attn_fn(q, k, v, lam)
    o = rmsnorm(o, g_subln) * (1.0 - lam_init)
    a_out = o.reshape(B, S, ATTN_WIDTH) @ w_br_a
    u = cc * cx
    u_pad = jnp.concatenate([conv_prev.astype(u.dtype), u], axis=1)
    c_out = (cb * causal_short_conv(u_pad, w_conv, S)) @ w_br_c
    merged = jax.nn.sigmoid(ga) * a_out + jax.nn.sigmoid(gc) * c_out
    x = x + merged @ w_out
    g_, u_ = jnp.split(rmsnorm(x, g_ffn) @ w_gu, 2, axis=-1)
    x = x + (jax.nn.silu(g_) * u_) @ w_down
    return x, k.reshape(B, S, ATTN_HEADS, HEAD_QK), v, u_pad[:, -(CONV_WIDTH - 1):]


def setup_inputs(seed: int = 0) -> dict:
    key = jax.random.key(seed)
    ks = jax.random.split(key, 24)
    n_pages = PAST_LEN // PAGE_SIZE
    n_used = DEC_BATCH * n_pages
    n_pool = n_used + max(1, n_used // 4)

    def nrm(k, shape, scale):
        return jax.random.normal(k, shape, jnp.float32) * scale

    page_table = jax.random.permutation(ks[5], n_pool)[:n_used].reshape(DEC_BATCH, n_pages).astype(jnp.int32)
    return {
        "x_prompt": nrm(ks[0], (BATCH, SEQ, D_MODEL), 1.0),
        "x_sample": nrm(ks[1], (DEC_BATCH, DEC_SEQ, D_MODEL), 1.0),
        "cache_k": nrm(ks[2], (DEPTH, n_pool, PAGE_SIZE, ATTN_HEADS, HEAD_QK), 1.0),
        "cache_v": nrm(ks[3], (DEPTH, n_pool, PAGE_SIZE, ATTN_HEADS, HEAD_V), 1.0),
        "state_conv": nrm(ks[4], (DEPTH, DEC_BATCH, CONV_WIDTH - 1, CONV_DIM), 1.0),
        "page_table": page_table,
        "w_in": nrm(ks[6], (DEPTH, D_MODEL, IN_COLS), D_MODEL ** -0.5),
        "w_conv": nrm(ks[7], (DEPTH, CONV_WIDTH, CONV_DIM), CONV_WIDTH ** -0.5),
        "lambda_q1": nrm(ks[8], (DEPTH, QK_DIM), 0.1),
        "lambda_k1": nrm(ks[9], (DEPTH, QK_DIM), 0.1),
        "lambda_q2": nrm(ks[10], (DEPTH, QK_DIM), 0.1),
        "lambda_k2": nrm(ks[11], (DEPTH, QK_DIM), 0.1),
        "g_subln": 1.0 + nrm(ks[12], (DEPTH, HEAD_V), 0.02),
        "w_branch_attn": nrm(ks[13], (DEPTH, ATTN_WIDTH, D_MODEL), ATTN_WIDTH ** -0.5),
        "w_branch_conv": nrm(ks[14], (DEPTH, CONV_DIM, D_MODEL), CONV_DIM ** -0.5),
        "w_out": nrm(ks[15], (DEPTH, D_MODEL, D_MODEL), D_MODEL ** -0.5),
        "g_mix": 1.0 + nrm(ks[16], (DEPTH, D_MODEL), 0.02),
        "g_ffn": 1.0 + nrm(ks[17], (DEPTH, D_MODEL), 0.02),
        "w_gate_up": nrm(ks[18], (DEPTH, D_MODEL, 2 * D_FF), D_MODEL ** -0.5),
        "w_down": nrm(ks[19], (DEPTH, D_FF, D_MODEL), D_FF ** -0.5),
        "g_final": 1.0 + nrm(ks[20], (D_MODEL,), 0.02),
    }


def reference(x_prompt, x_sample, cache_k, cache_v, state_conv, page_table, w_in, w_conv,
              lambda_q1, lambda_k1, lambda_q2, lambda_k2, g_subln, w_branch_attn, w_branch_conv,
              w_out, g_mix, g_ffn, w_gate_up, w_down, g_final):
    Bp, Sp = x_prompt.shape[0], x_prompt.shape[1]
    Bd, Sd = x_sample.shape[0], x_sample.shape[1]
    past = page_table.shape[1] * PAGE_SIZE
    pos_p = jnp.arange(Sp, dtype=jnp.int32)
    pos_s = past + jnp.arange(Sd, dtype=jnp.int32)
    xp, xs = x_prompt, x_sample
    kp_l, vp_l, cp_l, ks_l, vs_l, cs_l = [], [], [], [], [], []
    for l in range(DEPTH):
        weights = (w_in[l], w_conv[l], lambda_q1[l], lambda_k1[l], lambda_q2[l], lambda_k2[l],
                   g_subln[l], w_branch_attn[l], w_branch_conv[l], w_out[l], g_mix[l], g_ffn[l],
                   w_gate_up[l], w_down[l])
        conv0 = jnp.zeros((Bp, CONV_WIDTH - 1, CONV_DIM), xp.dtype)
        xp, kp, vp, cp = decoder_layer(xp, pos_p, conv0, attend_prompt, l, *weights)
        k_past = cache_k[l][page_table].reshape(Bd, past, ATTN_HEADS, 2, QK_DIM)
        v_past = cache_v[l][page_table].reshape(Bd, past, ATTN_HEADS, HEAD_V)
        attn_s = functools.partial(attend_sample, k_past=k_past, v_past=v_past)
        xs, ks_, vs_, cs_ = decoder_layer(xs, pos_s, state_conv[l], attn_s, l, *weights)
        kp_l.append(kp); vp_l.append(vp); cp_l.append(cp)
        ks_l.append(ks_); vs_l.append(vs_); cs_l.append(cs_)
    y_prompt = rmsnorm(xp, g_final)
    y_sample = rmsnorm(xs, g_final)
    return (y_prompt, y_sample, jnp.stack(kp_l), jnp.stack(vp_l), jnp.stack(cp_l),
            jnp.stack(ks_l), jnp.stack(vs_l), jnp.stack(cs_l))
```

```python
import functools
import math

import jax
import jax.numpy as jnp
from jax import lax
from jax.experimental import pallas as pl
from jax.experimental.pallas import tpu as pltpu

F32 = jnp.float32
BF16 = jnp.bfloat16

D_MODEL = 2048
ATTN_HEADS = 8
QK_DIM = 64
HEAD_V = 128
ATTN_WIDTH = ATTN_HEADS * HEAD_V
CONV_DIM = D_MODEL // 2
CONV_WIDTH = 3
D_FF = 5632
ROPE_THETA = 10000.0
PAGE_SIZE = 128
EPS = 1e-5
SCALE = QK_DIM ** -0.5
NEG_INF = -1e30
LANES = 128

VMEM_LIMIT = 56 * 1024 * 1024


def _cparams(*sem):
    return pltpu.CompilerParams(dimension_semantics=sem, vmem_limit_bytes=VMEM_LIMIT)


def _const_spec(shape):
    nd = len(shape)
    return pl.BlockSpec(shape, lambda *_: (0,) * nd, pipeline_mode=pl.Buffered(1))


def _rms(x, g):
    return x * lax.rsqrt(jnp.mean(x * x, axis=-1, keepdims=True) + EPS) * g


def _norm_kernel(x_ref, g_ref, h_ref):
    h_ref[...] = _rms(x_ref[...], g_ref[...]).astype(BF16)


def _norm(x, g, tm):
    m = x.shape[0]
    return pl.pallas_call(
        _norm_kernel,
        grid=(m // tm,),
        in_specs=[pl.BlockSpec((tm, D_MODEL), lambda i: (i, 0)), _const_spec((1, D_MODEL))],
        out_specs=pl.BlockSpec((tm, D_MODEL), lambda i: (i, 0)),
        out_shape=jax.ShapeDtypeStruct((m, D_MODEL), BF16),
        compiler_params=_cparams("parallel"),
        name="norm_in",
    )(x, g)


def _rope(z, cos, sin_signed, first_half):
    partner = jnp.where(first_half, pltpu.roll(z, 96, 1), pltpu.roll(z, 32, 1))
    return z * cos + partner * sin_signed


def _qkv_kernel(h_ref, w_ref, cos_ref, sin_ref, qq_ref, k_ref, kb_ref, v_ref, vb_ref):
    z = jnp.dot(h_ref[...], w_ref[...], preferred_element_type=F32)
    cos = cos_ref[...]
    sin = sin_ref[...]
    lane = lax.broadcasted_iota(jnp.int32, cos.shape, 1)
    first_half = (lane & 32) == 0
    map1 = lane < QK_DIM
    for hh in range(ATTN_HEADS):
        lo = hh * HEAD_V
        q = _rope(z[:, lo:lo + HEAD_V], cos, sin, first_half) * SCALE
        qq_ref[0, :, lo:lo + HEAD_V] = jnp.where(map1, q, 0.0).astype(BF16)
        qq_ref[1, :, lo:lo + HEAD_V] = jnp.where(map1, 0.0, q).astype(BF16)
        k = _rope(z[:, ATTN_WIDTH + lo:ATTN_WIDTH + lo + HEAD_V], cos, sin, first_half)
        k_ref[:, lo:lo + HEAD_V] = k
        kb_ref[:, lo:lo + HEAD_V] = k.astype(BF16)
    v = z[:, 2 * ATTN_WIDTH:]
    v_ref[...] = v
    vb_ref[...] = v.astype(BF16)


def _qkv(h, w, cos, sin_signed, tm):
    m = h.shape[0]
    nt = cos.shape[0] // tm
    row = lambda i: (i, 0)
    return pl.pallas_call(
        _qkv_kernel,
        grid=(m // tm,),
        in_specs=[pl.BlockSpec((tm, D_MODEL), row),
                  _const_spec((D_MODEL, 3 * ATTN_WIDTH)),
                  pl.BlockSpec((tm, LANES), lambda i: (i % nt, 0)),
                  pl.BlockSpec((tm, LANES), lambda i: (i % nt, 0))],
        out_specs=[pl.BlockSpec((2, tm, ATTN_WIDTH), lambda i: (0, i, 0)),
                   pl.BlockSpec((tm, ATTN_WIDTH), row), pl.BlockSpec((tm, ATTN_WIDTH), row),
                   pl.BlockSpec((tm, ATTN_WIDTH), row), pl.BlockSpec((tm, ATTN_WIDTH), row)],
        out_shape=[jax.ShapeDtypeStruct((2, m, ATTN_WIDTH), BF16),
                   jax.ShapeDtypeStruct((m, ATTN_WIDTH), F32), jax.ShapeDtypeStruct((m, ATTN_WIDTH), BF16),
                   jax.ShapeDtypeStruct((m, ATTN_WIDTH), F32), jax.ShapeDtypeStruct((m, ATTN_WIDTH), BF16)],
        compiler_params=_cparams("parallel"),
        name="qkv_rope",
    )(h, w, cos, sin_signed)


HALO = 8


def _conv_prompt_kernel(tiles_per_seq, h_ref, w_ref, wc_ref, c_ref, st_ref, u_scr):
    i = pl.program_id(0)
    tm = h_ref.shape[0]
    z = jnp.dot(h_ref[...], w_ref[...], preferred_element_type=F32)
    u = z[:, CONV_DIM:2 * CONV_DIM] * z[:, 2 * CONV_DIM:]

    @pl.when(i % tiles_per_seq == 0)
    def _():
        u_scr[0:HALO, :] = jnp.zeros((HALO, CONV_DIM), F32)

    @pl.when(i % tiles_per_seq != 0)
    def _():
        u_scr[0:HALO, :] = u_scr[tm:tm + HALO, :]

    u_scr[HALO:HALO + tm, :] = u
    wc = wc_ref[...]
    y = (wc[0:1, :] * u_scr[HALO - 2:HALO - 2 + tm, :]
         + wc[1:2, :] * u_scr[HALO - 1:HALO - 1 + tm, :]
         + wc[2:3, :] * u)
    c_ref[...] = (z[:, :CONV_DIM] * y).astype(BF16)
    st_ref[...] = u[tm - HALO:, :]


def _conv_prompt(h, w, w_conv, tm, seq):
    m = h.shape[0]
    tiles_per_seq = seq // tm
    return pl.pallas_call(
        functools.partial(_conv_prompt_kernel, tiles_per_seq),
        grid=(m // tm,),
        in_specs=[pl.BlockSpec((tm, D_MODEL), lambda i: (i, 0)),
                  _const_spec((D_MODEL, 3 * CONV_DIM)),
                  _const_spec((CONV_WIDTH, CONV_DIM))],
        out_specs=[pl.BlockSpec((tm, CONV_DIM), lambda i: (i, 0)),
                   pl.BlockSpec((None, HALO, CONV_DIM), lambda i: (i // tiles_per_seq, 0, 0))],
        out_shape=[jax.ShapeDtypeStruct((m, CONV_DIM), BF16),
                   jax.ShapeDtypeStruct((m // seq, HALO, CONV_DIM), F32)],
        scratch_shapes=[pltpu.VMEM((tm + HALO, CONV_DIM), F32)],
        compiler_params=_cparams("arbitrary"),
        name="conv_prompt",
    )(h, w, w_conv)


def _conv_sample_kernel(h_ref, w_ref, wc_ref, st_ref, c_ref, nst_ref):
    z = jnp.dot(h_ref[...], w_ref[...], preferred_element_type=F32)
    u = z[:, CONV_DIM:2 * CONV_DIM] * z[:, 2 * CONV_DIM:]
    wc = wc_ref[...]
    s0 = st_ref[:, :CONV_DIM]
    s1 = st_ref[:, CONV_DIM:]
    y = wc[0:1, :] * s0 + wc[1:2, :] * s1 + wc[2:3, :] * u
    c_ref[...] = (z[:, :CONV_DIM] * y).astype(BF16)
    nst_ref[:, :CONV_DIM] = s1
    nst_ref[:, CONV_DIM:] = u


def _conv_sample(h, w, w_conv, state2d):
    m = h.shape[0]
    return pl.pallas_call(
        _conv_sample_kernel,
        out_shape=[jax.ShapeDtypeStruct((m, CONV_DIM), BF16),
                   jax.ShapeDtypeStruct((m, 2 * CONV_DIM), F32)],
        compiler_params=pltpu.CompilerParams(vmem_limit_bytes=VMEM_LIMIT),
        name="conv_sample",
    )(h, w, w_conv, state2d)


def _gate_kernel(h_ref, w_ref, ga_ref, gc_ref):
    z = jnp.dot(h_ref[...], w_ref[...], preferred_element_type=F32)
    s = jax.nn.sigmoid(z)
    ga_ref[...] = s[:, :D_MODEL].astype(BF16)
    gc_ref[...] = s[:, D_MODEL:].astype(BF16)


def _gates(h, w, tm):
    m = h.shape[0]
    row = lambda i: (i, 0)
    return pl.pallas_call(
        _gate_kernel,
        grid=(m // tm,),
        in_specs=[pl.BlockSpec((tm, D_MODEL), row), _const_spec((D_MODEL, 2 * D_MODEL))],
        out_specs=[pl.BlockSpec((tm, D_MODEL), row), pl.BlockSpec((tm, D_MODEL), row)],
        out_shape=[jax.ShapeDtypeStruct((m, D_MODEL), BF16), jax.ShapeDtypeStruct((m, D_MODEL), BF16)],
        compiler_params=_cparams("parallel"),
        name="gates",
    )(h, w)


def _lam(lam_ref, lam_init):
    lv = lam_ref[...]
    a = jnp.sum(lv[0:1, :] * lv[1:2, :], axis=-1, keepdims=True)
    b = jnp.sum(lv[2:3, :] * lv[3:4, :], axis=-1, keepdims=True)
    return jnp.exp(a) - jnp.exp(b) + lam_init


def _subln(o, g, lam_init):
    return o * lax.rsqrt(jnp.mean(o * o, axis=-1, keepdims=True) + EPS) * g * (1.0 - lam_init)


def _attn_prompt_kernel(lam_init, qq_ref, k_ref, v_ref, lam_ref, g_ref, o_ref, m_scr, l_scr, acc_scr):
    i = pl.program_id(2)
    j = pl.program_id(3)
    tq = qq_ref.shape[1]
    tk = k_ref.shape[0]

    @pl.when(j == 0)
    def _():
        m_scr[...] = jnp.full(m_scr.shape, NEG_INF, F32)
        l_scr[...] = jnp.zeros(l_scr.shape, F32)
        acc_scr[...] = jnp.zeros(acc_scr.shape, F32)

    @pl.when(j <= i)
    def _():
        q = jnp.concatenate([qq_ref[0], qq_ref[1]], axis=0)
        s = lax.dot_general(q, k_ref[...], (((1,), (1,)), ((), ())), preferred_element_type=F32)
        qpos = i * tq + lax.broadcasted_iota(jnp.int32, (tq, tk), 0)
        kpos = j * tk + lax.broadcasted_iota(jnp.int32, (tq, tk), 1)
        keep = jnp.concatenate([kpos <= qpos] * 2, axis=0)
        s = jnp.where(keep, s, NEG_INF)
        m_prev = m_scr[...]
        m_new = jnp.maximum(m_prev, jnp.max(s, axis=-1, keepdims=True))
        alpha = jnp.exp(m_prev - m_new)
        p = jnp.exp(s - m_new)
        l_scr[...] = alpha * l_scr[...] + jnp.sum(p, axis=-1, keepdims=True)
        acc_scr[...] = alpha * acc_scr[...] + jnp.dot(p.astype(BF16), v_ref[...], preferred_element_type=F32)
        m_scr[...] = m_new

    @pl.when(j == i)
    def _():
        o12 = acc_scr[...] / l_scr[...]
        o = o12[:tq] - _lam(lam_ref, lam_init) * o12[tq:]
        o_ref[...] = _subln(o, g_ref[...], lam_init).astype(BF16)


def _attn_prompt(qq, kb, vb, lam_vecs, g_subln, lam_init, batch, seq, tq):
    m = kb.shape[0]
    nq = seq // tq
    return pl.pallas_call(
        functools.partial(_attn_prompt_kernel, lam_init),
        grid=(batch, ATTN_HEADS, nq, nq),
        in_specs=[pl.BlockSpec((2, tq, HEAD_V), lambda b, h, i, j: (0, b * nq + i, h)),
                  pl.BlockSpec((tq, HEAD_V), lambda b, h, i, j: (b * nq + jnp.minimum(i, j), h)),
                  pl.BlockSpec((tq, HEAD_V), lambda b, h, i, j: (b * nq + jnp.minimum(i, j), h)),
                  pl.BlockSpec((8, QK_DIM), lambda b, h, i, j: (0, 0)),
                  pl.BlockSpec((1, HEAD_V), lambda b, h, i, j: (0, 0))],
        out_specs=pl.BlockSpec((tq, HEAD_V), lambda b, h, i, j: (b * nq + i, h)),
        out_shape=jax.ShapeDtypeStruct((m, ATTN_WIDTH), BF16),
        scratch_shapes=[pltpu.VMEM((2 * tq, 1), F32), pltpu.VMEM((2 * tq, 1), F32),
                        pltpu.VMEM((2 * tq, HEAD_V), F32)],
        compiler_params=_cparams("parallel", "parallel", "arbitrary", "arbitrary"),
        name="attn_prompt",
    )(qq, kb, vb, lam_vecs, g_subln)


PAGES_PER_STEP = 8
ROWS = 2 * ATTN_HEADS


PAGE_ROWS = PAGE_SIZE * ATTN_HEADS


def _attn_sample_kernel(lam_init, *refs):
    n = PAGES_PER_STEP
    pt_ref, qd_ref, kn_ref, vn_ref, lam_ref, g_ref = refs[:6]
    k_refs = refs[6:6 + n]
    v_refs = refs[6 + n:6 + 2 * n]
    o_ref = refs[6 + 2 * n]
    m_scr, l_scr, acc_scr = refs[7 + 2 * n:]
    del pt_ref
    t = pl.program_id(1)
    qd = qd_ref[...]

    @pl.when(t == 0)
    def _():
        m_scr[...] = jnp.sum(qd.astype(F32) * kn_ref[...], axis=-1, keepdims=True)
        l_scr[...] = jnp.ones(l_scr.shape, F32)
        acc_scr[...] = vn_ref[...]

    s = jnp.concatenate(
        [lax.dot_general(qd, k_refs[p][...].astype(BF16), (((1,), (1,)), ((), ())),
                         preferred_element_type=F32) for p in range(n)], axis=1)
    col = lax.broadcasted_iota(jnp.int32, s.shape, 1)
    row = lax.broadcasted_iota(jnp.int32, s.shape, 0)
    s = jnp.where((col % ATTN_HEADS) == (row % ATTN_HEADS), s, NEG_INF)
    m_prev = m_scr[...]
    m_new = jnp.maximum(m_prev, jnp.max(s, axis=-1, keepdims=True))
    alpha = jnp.exp(m_prev - m_new)
    p_all = jnp.exp(s - m_new)
    l_scr[...] = alpha * l_scr[...] + jnp.sum(p_all, axis=-1, keepdims=True)
    pb = p_all.astype(BF16)
    pv = jnp.dot(pb[:, 0:PAGE_ROWS], v_refs[0][...].astype(BF16), preferred_element_type=F32)
    for p in range(1, n):
        pv += jnp.dot(pb[:, p * PAGE_ROWS:(p + 1) * PAGE_ROWS], v_refs[p][...].astype(BF16),
                      preferred_element_type=F32)
    acc_scr[...] = alpha * acc_scr[...] + pv
    m_scr[...] = m_new

    @pl.when(t == pl.num_programs(1) - 1)
    def _():
        o12 = acc_scr[...] / l_scr[...]
        o = o12[:ATTN_HEADS] - _lam(lam_ref, lam_init) * o12[ATTN_HEADS:]
        o_ref[...] = _subln(o, g_ref[...], lam_init).astype(BF16)


def _attn_sample(page_table, qd, k_new, v_new, lam_vecs, g_subln, ck, cv, lam_init):
    bd, n_pages = page_table.shape
    n = PAGES_PER_STEP
    steps = n_pages // n

    def page_spec(p):
        return pl.BlockSpec((None, PAGE_ROWS, HEAD_V), lambda b, t, pt: (pt[b, t * n + p], 0, 0))

    rowb = lambda b, t, pt: (b, 0, 0)
    grid_spec = pltpu.PrefetchScalarGridSpec(
        num_scalar_prefetch=1,
        grid=(bd, steps),
        in_specs=[pl.BlockSpec((None, ROWS, HEAD_V), rowb),
                  pl.BlockSpec((None, ROWS, HEAD_V), rowb),
                  pl.BlockSpec((None, ROWS, HEAD_V), rowb),
                  pl.BlockSpec((8, QK_DIM), lambda b, t, pt: (0, 0)),
                  pl.BlockSpec((1, HEAD_V), lambda b, t, pt: (0, 0))]
                 + [page_spec(p) for p in range(n)] + [page_spec(p) for p in range(n)],
        out_specs=pl.BlockSpec((None, ATTN_HEADS, HEAD_V), rowb),
        scratch_shapes=[pltpu.VMEM((ROWS, 1), F32), pltpu.VMEM((ROWS, 1), F32),
                        pltpu.VMEM((ROWS, HEAD_V), F32)],
    )
    return pl.pallas_call(
        functools.partial(_attn_sample_kernel, lam_init),
        grid_spec=grid_spec,
        out_shape=jax.ShapeDtypeStruct((bd, ATTN_HEADS, HEAD_V), BF16),
        compiler_params=_cparams("parallel", "arbitrary"),
        name="attn_sample",
    )(page_table, qd, k_new, v_new, lam_vecs, g_subln, *([ck] * n), *([cv] * n))


def _merge_kernel(o_ref, c_ref, ga_ref, gc_ref, x_ref, wa_ref, wc_ref, wo_ref, g_ref, x1_ref, h2_ref):
    a_out = jnp.dot(o_ref[...], wa_ref[...], preferred_element_type=F32)
    c_out = jnp.dot(c_ref[...], wc_ref[...], preferred_element_type=F32)
    merged = ga_ref[...].astype(F32) * a_out + gc_ref[...].astype(F32) * c_out
    x1 = x_ref[...] + jnp.dot(merged.astype(BF16), wo_ref[...], preferred_element_type=F32)
    x1_ref[...] = x1
    h2_ref[...] = _rms(x1, g_ref[...]).astype(BF16)


def _merge(o, c, ga, gc, x, wa, wc, wo, g_ffn, tm):
    m = x.shape[0]
    row = lambda i: (i, 0)
    return pl.pallas_call(
        _merge_kernel,
        grid=(m // tm,),
        in_specs=[pl.BlockSpec((tm, ATTN_WIDTH), row), pl.BlockSpec((tm, CONV_DIM), row),
                  pl.BlockSpec((tm, D_MODEL), row), pl.BlockSpec((tm, D_MODEL), row),
                  pl.BlockSpec((tm, D_MODEL), row),
                  _const_spec((ATTN_WIDTH, D_MODEL)), _const_spec((CONV_DIM, D_MODEL)),
                  _const_spec((D_MODEL, D_MODEL)), _const_spec((1, D_MODEL))],
        out_specs=[pl.BlockSpec((tm, D_MODEL), row), pl.BlockSpec((tm, D_MODEL), row)],
        out_shape=[jax.ShapeDtypeStruct((m, D_MODEL), F32), jax.ShapeDtypeStruct((m, D_MODEL), BF16)],
        compiler_params=_cparams("parallel"),
        name="merge_out",
    )(o, c, ga, gc, x, wa, wc, wo, g_ffn)


FF_BLOCK = 512


def _ffn_kernel(h_ref, x1_ref, wg_ref, wu_ref, wd_ref, gf_ref, y_ref, acc_scr):
    j = pl.program_id(1)
    h = h_ref[...]
    g = jnp.dot(h, wg_ref[...], preferred_element_type=F32)
    u = jnp.dot(h, wu_ref[...], preferred_element_type=F32)
    act = (g * jax.nn.sigmoid(g) * u).astype(BF16)
    part = jnp.dot(act, wd_ref[...], preferred_element_type=F32)

    @pl.when(j == 0)
    def _():
        acc_scr[...] = part

    @pl.when(j != 0)
    def _():
        acc_scr[...] += part

    @pl.when(j == pl.num_programs(1) - 1)
    def _():
        y_ref[...] = _rms(x1_ref[...] + acc_scr[...], gf_ref[...])


def _ffn(h2, x1, w_gu, w_down, g_final, tm):
    m = h2.shape[0]
    nj = D_FF // FF_BLOCK
    row = lambda i, j: (i, 0)
    return pl.pallas_call(
        _ffn_kernel,
        grid=(m // tm, nj),
        in_specs=[pl.BlockSpec((tm, D_MODEL), row), pl.BlockSpec((tm, D_MODEL), row),
                  pl.BlockSpec((D_MODEL, FF_BLOCK), lambda i, j: (0, j)),
                  pl.BlockSpec((D_MODEL, FF_BLOCK), lambda i, j: (0, nj + j)),
                  pl.BlockSpec((FF_BLOCK, D_MODEL), lambda i, j: (j, 0)),
                  pl.BlockSpec((1, D_MODEL), lambda i, j: (0, 0))],
        out_specs=pl.BlockSpec((tm, D_MODEL), row),
        out_shape=jax.ShapeDtypeStruct((m, D_MODEL), F32),
        scratch_shapes=[pltpu.VMEM((tm, D_MODEL), F32)],
        compiler_params=_cparams("parallel", "arbitrary"),
        name="ffn",
    )(h2, x1, w_gu, w_gu, w_down, g_final)


def _rope_tables(pos):
    half = QK_DIM // 2
    inv = ROPE_THETA ** (-jnp.arange(half, dtype=F32) / half)
    ang = pos.astype(F32)[:, None] * inv[None, :]
    cos = jnp.tile(jnp.cos(ang), (1, LANES // half))
    sin = jnp.tile(jnp.sin(ang), (1, LANES // half))
    first_half = (jnp.arange(LANES) & half) == 0
    return cos, jnp.where(first_half[None, :], -sin, sin)


def kernel(x_prompt, x_sample, cache_k, cache_v, state_conv, page_table, w_in, w_conv, lambda_q1, lambda_k1, lambda_q2, lambda_k2, g_subln, w_branch_attn, w_branch_conv, w_out, g_mix, g_ffn, w_gate_up, w_down, g_final):
    bp, sp, _ = x_prompt.shape
    bd, sd, _ = x_sample.shape
    assert sd == 1 and w_in.shape[0] == 1
    n_pool = cache_k.shape[1]
    past = page_table.shape[1] * PAGE_SIZE
    lam_init = 0.8 - 0.6 * math.exp(-0.3 * 0)

    w_in_b = w_in[0].astype(BF16)
    w_qkv = w_in_b[:, :3 * ATTN_WIDTH]
    w_cv = w_in_b[:, 3 * ATTN_WIDTH:3 * ATTN_WIDTH + 3 * CONV_DIM]
    w_gt = w_in_b[:, 3 * ATTN_WIDTH + 3 * CONV_DIM:]
    wa = w_branch_attn[0].astype(BF16)
    wc = w_branch_conv[0].astype(BF16)
    wo = w_out[0].astype(BF16)
    w_gu = w_gate_up[0].astype(BF16)
    w_dn = w_down[0].astype(BF16)
    g_mix2 = g_mix[0].reshape(1, D_MODEL)
    g_ffn2 = g_ffn[0].reshape(1, D_MODEL)
    g_fin2 = g_final.reshape(1, D_MODEL)
    g_sub2 = g_subln[0].reshape(1, HEAD_V)
    lam_vecs = jnp.concatenate([lambda_q1, lambda_k1, lambda_q2, lambda_k2, jnp.zeros((4, QK_DIM), F32)], axis=0)
    wcv3 = w_conv[0]

    mp = bp * sp
    xp = x_prompt.reshape(mp, D_MODEL)
    cos_p, sin_p = _rope_tables(jnp.arange(sp, dtype=jnp.int32))
    tm = 512
    hp = _norm(xp, g_mix2, tm)
    qq, k_p, kb, v_p, vb = _qkv(hp, w_qkv, cos_p, sin_p, tm)
    c_p, st_p = _conv_prompt(hp, w_cv, wcv3, tm, sp)
    ga_p, gc_p = _gates(hp, w_gt, tm)
    o_p = _attn_prompt(qq, kb, vb, lam_vecs, g_sub2, lam_init, bp, sp, 512)
    x1_p, h2_p = _merge(o_p, c_p, ga_p, gc_p, xp, wa, wc, wo, g_ffn2, 256)
    y_p = _ffn(h2_p, x1_p, w_gu, w_dn, g_fin2, tm)

    xs = x_sample.reshape(bd, D_MODEL)
    cos_s, sin_s = _rope_tables(jnp.full((bd,), past, dtype=jnp.int32))
    hs = _norm(xs, g_mix2, bd)
    qq_s, k_s, _, v_s, _ = _qkv(hs, w_qkv, cos_s, sin_s, bd)
    c_s, st_s = _conv_sample(hs, w_cv, wcv3, state_conv[0].reshape(bd, 2 * CONV_DIM))
    ga_s, gc_s = _gates(hs, w_gt, bd)
    qd = jnp.transpose(qq_s.reshape(2, bd, ATTN_HEADS, HEAD_V), (1, 0, 2, 3)).reshape(bd, ROWS, HEAD_V)
    k_new = jnp.tile(k_s.reshape(bd, ATTN_HEADS, HEAD_V), (1, 2, 1))
    v_new = jnp.tile(v_s.reshape(bd, ATTN_HEADS, HEAD_V), (1, 2, 1))
    o_s = _attn_sample(page_table, qd, k_new, v_new, lam_vecs, g_sub2,
                       cache_k.reshape(n_pool, PAGE_ROWS, HEAD_V),
                       cache_v.reshape(n_pool, PAGE_ROWS, HEAD_V), lam_init)
    x1_s, h2_s = _merge(o_s.reshape(bd, ATTN_WIDTH), c_s, ga_s, gc_s, xs, wa, wc, wo, g_ffn2, bd)
    y_s = _ffn(h2_s, x1_s, w_gu, w_dn, g_fin2, bd)

    return (y_p.reshape(bp, sp, D_MODEL),
            y_s.reshape(bd, sd, D_MODEL),
            k_p.reshape(1, bp, sp, ATTN_HEADS, HEAD_V),
            v_p.reshape(1, bp, sp, ATTN_HEADS, HEAD_V),
            st_p[:, HALO - 2:, :].reshape(1, bp, CONV_WIDTH - 1, CONV_DIM),
            k_s.reshape(1, bd, sd, ATTN_HEADS, HEAD_V),
            v_s.reshape(1, bd, sd, ATTN_HEADS, HEAD_V),
            st_s.reshape(1, bd, CONV_WIDTH - 1, CONV_DIM))
```

```python
import functools
import math

import jax
import jax.numpy as jnp
from jax import lax
from jax.experimental import pallas as pl
from jax.experimental.pallas import tpu as pltpu

F32 = jnp.float32
BF16 = jnp.bfloat16

D_MODEL = 2048
ATTN_HEADS = 8
QK_DIM = 64
HEAD_V = 128
ATTN_WIDTH = ATTN_HEADS * HEAD_V
CONV_DIM = D_MODEL // 2
CONV_WIDTH = 3
D_FF = 5632
ROPE_THETA = 10000.0
PAGE_SIZE = 128
EPS = 1e-5
SCALE = QK_DIM ** -0.5
Q_SCALE = SCALE * math.log2(math.e)
NEG_INF = -1e30
LANES = 128

VMEM_LIMIT = 56 * 1024 * 1024


def _cparams(*sem):
    return pltpu.CompilerParams(dimension_semantics=sem, vmem_limit_bytes=VMEM_LIMIT)


def _const_spec(shape):
    nd = len(shape)
    return pl.BlockSpec(shape, lambda *_: (0,) * nd, pipeline_mode=pl.Buffered(1))


def _col_block_spec(rows, width, block):
    return pl.BlockSpec((rows, width), lambda *_: (0, block), pipeline_mode=pl.Buffered(1))


QKV_BLOCK = (3 * ATTN_WIDTH, 0)
CONV_BLOCK = (3 * CONV_DIM, 1)
GATE_A_BLOCK = (D_MODEL, 3)
GATE_C_BLOCK = (D_MODEL, 4)
assert QKV_BLOCK[0] == CONV_BLOCK[0] and GATE_A_BLOCK[1] * D_MODEL == QKV_BLOCK[0] + CONV_BLOCK[0]


def _rms(x, g):
    return x * lax.rsqrt(jnp.mean(x * x, axis=-1, keepdims=True) + EPS) * g


def _norm_kernel(x_ref, g_ref, h_ref):
    h_ref[...] = _rms(x_ref[...], g_ref[...]).astype(BF16)


def _norm(x, g, tm):
    m = x.shape[0]
    return pl.pallas_call(
        _norm_kernel,
        grid=(m // tm,),
        in_specs=[pl.BlockSpec((tm, D_MODEL), lambda i: (i, 0)), _const_spec((1, D_MODEL))],
        out_specs=pl.BlockSpec((tm, D_MODEL), lambda i: (i, 0)),
        out_shape=jax.ShapeDtypeStruct((m, D_MODEL), BF16),
        compiler_params=_cparams("parallel"),
        name="norm_in",
    )(x, g)


def _rope(z, cos, sin_signed, first_half):
    partner = jnp.where(first_half, pltpu.roll(z, 96, 1), pltpu.roll(z, 32, 1))
    return z * cos + partner * sin_signed


def _qkv_kernel(h_ref, w_ref, cos_ref, sin_ref, qq_ref, k_ref, kb_ref, v_ref, vb_ref):
    z = jnp.dot(h_ref[...], w_ref[...], preferred_element_type=F32)
    cos = cos_ref[...]
    sin = sin_ref[...]
    lane = lax.broadcasted_iota(jnp.int32, cos.shape, 1)
    first_half = (lane & 32) == 0
    map1 = lane < QK_DIM
    for hh in range(ATTN_HEADS):
        lo = hh * HEAD_V
        q = _rope(z[:, lo:lo + HEAD_V], cos, sin, first_half) * Q_SCALE
        qq_ref[0, :, lo:lo + HEAD_V] = jnp.where(map1, q, 0.0).astype(BF16)
        qq_ref[1, :, lo:lo + HEAD_V] = jnp.where(map1, 0.0, q).astype(BF16)
        k = _rope(z[:, ATTN_WIDTH + lo:ATTN_WIDTH + lo + HEAD_V], cos, sin, first_half)
        k_ref[:, lo:lo + HEAD_V] = k
        kb_ref[:, lo:lo + HEAD_V] = k.astype(BF16)
    v = z[:, 2 * ATTN_WIDTH:]
    v_ref[...] = v
    vb_ref[...] = v.astype(BF16)


def _qkv(h, w, cos, sin_signed, tm):
    m = h.shape[0]
    nt = cos.shape[0] // tm
    row = lambda i: (i, 0)
    return pl.pallas_call(
        _qkv_kernel,
        grid=(m // tm,),
        in_specs=[pl.BlockSpec((tm, D_MODEL), row),
                  _col_block_spec(D_MODEL, *QKV_BLOCK),
                  pl.BlockSpec((tm, LANES), lambda i: (i % nt, 0)),
                  pl.BlockSpec((tm, LANES), lambda i: (i % nt, 0))],
        out_specs=[pl.BlockSpec((2, tm, ATTN_WIDTH), lambda i: (0, i, 0)),
                   pl.BlockSpec((tm, ATTN_WIDTH), row), pl.BlockSpec((tm, ATTN_WIDTH), row),
                   pl.BlockSpec((tm, ATTN_WIDTH), row), pl.BlockSpec((tm, ATTN_WIDTH), row)],
        out_shape=[jax.ShapeDtypeStruct((2, m, ATTN_WIDTH), BF16),
                   jax.ShapeDtypeStruct((m, ATTN_WIDTH), F32), jax.ShapeDtypeStruct((m, ATTN_WIDTH), BF16),
                   jax.ShapeDtypeStruct((m, ATTN_WIDTH), F32), jax.ShapeDtypeStruct((m, ATTN_WIDTH), BF16)],
        compiler_params=_cparams("parallel"),
        name="qkv_rope",
    )(h, w, cos, sin_signed)


HALO = 8


def _conv_prompt_kernel(tiles_per_seq, h_ref, w_ref, wc_ref, c_ref, st_ref, u_scr):
    i = pl.program_id(0)
    tm = h_ref.shape[0]
    z = jnp.dot(h_ref[...], w_ref[...], preferred_element_type=F32)
    u = z[:, CONV_DIM:2 * CONV_DIM] * z[:, 2 * CONV_DIM:]

    @pl.when(i % tiles_per_seq == 0)
    def _():
        u_scr[0:HALO, :] = jnp.zeros((HALO, CONV_DIM), F32)

    @pl.when(i % tiles_per_seq != 0)
    def _():
        u_scr[0:HALO, :] = u_scr[tm:tm + HALO, :]

    u_scr[HALO:HALO + tm, :] = u
    wc = wc_ref[...]
    y = (wc[0:1, :] * u_scr[HALO - 2:HALO - 2 + tm, :]
         + wc[1:2, :] * u_scr[HALO - 1:HALO - 1 + tm, :]
         + wc[2:3, :] * u)
    c_ref[...] = (z[:, :CONV_DIM] * y).astype(BF16)
    st_ref[...] = u[tm - HALO:, :]


def _conv_prompt(h, w, w_conv, tm, seq):
    m = h.shape[0]
    tiles_per_seq = seq // tm
    return pl.pallas_call(
        functools.partial(_conv_prompt_kernel, tiles_per_seq),
        grid=(m // tm,),
        in_specs=[pl.BlockSpec((tm, D_MODEL), lambda i: (i, 0)),
                  _col_block_spec(D_MODEL, *CONV_BLOCK),
                  _const_spec((CONV_WIDTH, CONV_DIM))],
        out_specs=[pl.BlockSpec((tm, CONV_DIM), lambda i: (i, 0)),
                   pl.BlockSpec((None, HALO, CONV_DIM), lambda i: (i // tiles_per_seq, 0, 0))],
        out_shape=[jax.ShapeDtypeStruct((m, CONV_DIM), BF16),
                   jax.ShapeDtypeStruct((m // seq, HALO, CONV_DIM), F32)],
        scratch_shapes=[pltpu.VMEM((tm + HALO, CONV_DIM), F32)],
        compiler_params=_cparams("arbitrary"),
        name="conv_prompt",
    )(h, w, w_conv)


def _conv_sample_kernel(h_ref, w_ref, wc_ref, st_ref, c_ref, nst_ref):
    z = jnp.dot(h_ref[...], w_ref[...], preferred_element_type=F32)
    u = z[:, CONV_DIM:2 * CONV_DIM] * z[:, 2 * CONV_DIM:]
    wc = wc_ref[...]
    s0 = st_ref[:, :CONV_DIM]
    s1 = st_ref[:, CONV_DIM:]
    y = wc[0:1, :] * s0 + wc[1:2, :] * s1 + wc[2:3, :] * u
    c_ref[...] = (z[:, :CONV_DIM] * y).astype(BF16)
    nst_ref[:, :CONV_DIM] = s1
    nst_ref[:, CONV_DIM:] = u


def _conv_sample(h, w, w_conv, state2d):
    m = h.shape[0]
    return pl.pallas_call(
        _conv_sample_kernel,
        grid=(1,),
        in_specs=[_const_spec((m, D_MODEL)), _col_block_spec(D_MODEL, *CONV_BLOCK),
                  _const_spec((CONV_WIDTH, CONV_DIM)), _const_spec((m, 2 * CONV_DIM))],
        out_specs=[pl.BlockSpec((m, CONV_DIM), lambda i: (0, 0)),
                   pl.BlockSpec((m, 2 * CONV_DIM), lambda i: (0, 0))],
        out_shape=[jax.ShapeDtypeStruct((m, CONV_DIM), BF16),
                   jax.ShapeDtypeStruct((m, 2 * CONV_DIM), F32)],
        compiler_params=_cparams("arbitrary"),
        name="conv_sample",
    )(h, w, w_conv, state2d)


def _gate_kernel(h_ref, wa_ref, wc_ref, ga_ref, gc_ref):
    h = h_ref[...]
    ga_ref[...] = jax.nn.sigmoid(jnp.dot(h, wa_ref[...], preferred_element_type=F32)).astype(BF16)
    gc_ref[...] = jax.nn.sigmoid(jnp.dot(h, wc_ref[...], preferred_element_type=F32)).astype(BF16)


def _gates(h, w, tm):
    m = h.shape[0]
    row = lambda i: (i, 0)
    return pl.pallas_call(
        _gate_kernel,
        grid=(m // tm,),
        in_specs=[pl.BlockSpec((tm, D_MODEL), row),
                  _col_block_spec(D_MODEL, *GATE_A_BLOCK), _col_block_spec(D_MODEL, *GATE_C_BLOCK)],
        out_specs=[pl.BlockSpec((tm, D_MODEL), row), pl.BlockSpec((tm, D_MODEL), row)],
        out_shape=[jax.ShapeDtypeStruct((m, D_MODEL), BF16), jax.ShapeDtypeStruct((m, D_MODEL), BF16)],
        compiler_params=_cparams("parallel"),
        name="gates",
    )(h, w, w)


def _lam(lam_ref, lam_init):
    lv = lam_ref[...]
    a = jnp.sum(lv[0:1, :] * lv[1:2, :], axis=-1, keepdims=True)
    b = jnp.sum(lv[2:3, :] * lv[3:4, :], axis=-1, keepdims=True)
    return jnp.exp(a) - jnp.exp(b) + lam_init


def _subln(o, g, lam_init):
    return o * lax.rsqrt(jnp.mean(o * o, axis=-1, keepdims=True) + EPS) * g * (1.0 - lam_init)


ROW_BLOCK = 128


def _attn_tile(diag, qq_ref, k_ref, v_ref, m_scr, acc_scr):
    tq = qq_ref.shape[1]
    tk = k_ref.shape[0]
    k = k_ref[...]
    v_ext = jnp.concatenate([v_ref[...], jnp.ones((tk, HEAD_V), BF16)], axis=1)
    for mp in range(2):
        for r0 in range(0, tq, ROW_BLOCK):
            c0 = 0 if diag is None else diag * tk
            if r0 + ROW_BLOCK - 1 < c0:
                continue
            rows = pl.ds(mp * tq + r0, ROW_BLOCK)
            s = lax.dot_general(qq_ref[mp, r0:r0 + ROW_BLOCK, :], k, (((1,), (1,)), ((), ())),
                                preferred_element_type=F32)
            if diag is not None and r0 < c0 + tk - 1:
                qpos = r0 + lax.broadcasted_iota(jnp.int32, s.shape, 0)
                kpos = c0 + lax.broadcasted_iota(jnp.int32, s.shape, 1)
                s = jnp.where(kpos <= qpos, s, NEG_INF)
            m_prev = m_scr[rows, :]
            m_new = jnp.maximum(m_prev, jnp.max(s, axis=1, keepdims=True))
            p = jnp.exp2(s - jnp.tile(m_new, (1, tk // LANES)))
            alpha = jnp.exp2(m_prev - m_new)
            pv = jnp.dot(p.astype(BF16), v_ext, preferred_element_type=F32)
            acc_scr[rows, :] = jnp.tile(alpha, (1, 2)) * acc_scr[rows, :] + pv
            m_scr[rows, :] = m_new


def _attn_prompt_kernel(lam_init, it_ref, jt_ref, qq_ref, k_ref, v_ref, lam_ref, g_ref, o_ref, m_scr, acc_scr):
    step = pl.program_id(2)
    i = it_ref[step]
    j = jt_ref[step]
    tq = qq_ref.shape[1]
    ratio = tq // k_ref.shape[0]

    @pl.when(j == 0)
    def _():
        m_scr[...] = jnp.full(m_scr.shape, NEG_INF, F32)
        acc_scr[...] = jnp.zeros(acc_scr.shape, F32)

    @pl.when(j < ratio * i)
    def _():
        _attn_tile(None, qq_ref, k_ref, v_ref, m_scr, acc_scr)

    for d in range(ratio):
        @pl.when(j == ratio * i + d)
        def _(d=d):
            _attn_tile(d, qq_ref, k_ref, v_ref, m_scr, acc_scr)

    @pl.when(j == ratio * i + ratio - 1)
    def _():
        acc = acc_scr[...]
        o12 = acc[:, :HEAD_V] / acc[:, HEAD_V:]
        o = o12[:tq] - _lam(lam_ref, lam_init) * o12[tq:]
        o_ref[...] = _subln(o, g_ref[...], lam_init).astype(BF16)


def _attn_prompt(qq, kb, vb, lam_vecs, g_subln, lam_init, batch, seq, tq, tk):
    m = kb.shape[0]
    nq = seq // tq
    nk = seq // tk
    ratio = tq // tk
    pairs = [(i, j) for i in range(nq) for j in range(ratio * (i + 1))]
    i_tab = jnp.array([p[0] for p in pairs], jnp.int32)
    j_tab = jnp.array([p[1] for p in pairs], jnp.int32)
    grid_spec = pltpu.PrefetchScalarGridSpec(
        num_scalar_prefetch=2,
        grid=(batch, ATTN_HEADS, len(pairs)),
        in_specs=[pl.BlockSpec((2, tq, HEAD_V), lambda b, h, s, it, jt: (0, b * nq + it[s], h)),
                  pl.BlockSpec((tk, HEAD_V), lambda b, h, s, it, jt: (b * nk + jt[s], h)),
                  pl.BlockSpec((tk, HEAD_V), lambda b, h, s, it, jt: (b * nk + jt[s], h)),
                  pl.BlockSpec((8, QK_DIM), lambda b, h, s, it, jt: (0, 0)),
                  pl.BlockSpec((1, HEAD_V), lambda b, h, s, it, jt: (0, 0))],
        out_specs=pl.BlockSpec((tq, HEAD_V), lambda b, h, s, it, jt: (b * nq + it[s], h)),
        scratch_shapes=[pltpu.VMEM((2 * tq, LANES), F32), pltpu.VMEM((2 * tq, 2 * HEAD_V), F32)],
    )
    return pl.pallas_call(
        functools.partial(_attn_prompt_kernel, lam_init),
        grid_spec=grid_spec,
        out_shape=jax.ShapeDtypeStruct((m, ATTN_WIDTH), BF16),
        compiler_params=_cparams("parallel", "parallel", "arbitrary"),
        name="attn_prompt",
    )(i_tab, j_tab, qq, kb, vb, lam_vecs, g_subln)


PAGES_PER_STEP = 8
ROWS = 2 * ATTN_HEADS


PAGE_ROWS = PAGE_SIZE * ATTN_HEADS


def _attn_sample_kernel(lam_init, *refs):
    n = PAGES_PER_STEP
    pt_ref, qd_ref, kn_ref, vn_ref, lam_ref, g_ref = refs[:6]
    k_refs = refs[6:6 + n]
    v_refs = refs[6 + n:6 + 2 * n]
    o_ref = refs[6 + 2 * n]
    m_scr, l_scr, acc_scr = refs[7 + 2 * n:]
    del pt_ref
    t = pl.program_id(1)
    qd = qd_ref[...]

    @pl.when(t == 0)
    def _():
        m_scr[...] = jnp.sum(qd.astype(F32) * kn_ref[...], axis=-1, keepdims=True)
        l_scr[...] = jnp.ones(l_scr.shape, F32)
        acc_scr[...] = vn_ref[...]

    s = jnp.concatenate(
        [lax.dot_general(qd, k_refs[p][...].astype(BF16), (((1,), (1,)), ((), ())),
                         preferred_element_type=F32) for p in range(n)], axis=1)
    col = lax.broadcasted_iota(jnp.int32, s.shape, 1)
    row = lax.broadcasted_iota(jnp.int32, s.shape, 0)
    s = jnp.where((col % ATTN_HEADS) == (row % ATTN_HEADS), s, NEG_INF)
    m_prev = m_scr[...]
    m_new = jnp.maximum(m_prev, jnp.max(s, axis=-1, keepdims=True))
    alpha = jnp.exp2(m_prev - m_new)
    p_all = jnp.exp2(s - m_new)
    l_scr[...] = alpha * l_scr[...] + jnp.sum(p_all, axis=-1, keepdims=True)
    pb = p_all.astype(BF16)
    pv = jnp.dot(pb[:, 0:PAGE_ROWS], v_refs[0][...].astype(BF16), preferred_element_type=F32)
    for p in range(1, n):
        pv += jnp.dot(pb[:, p * PAGE_ROWS:(p + 1) * PAGE_ROWS], v_refs[p][...].astype(BF16),
                      preferred_element_type=F32)
    acc_scr[...] = alpha * acc_scr[...] + pv
    m_scr[...] = m_new

    @pl.when(t == pl.num_programs(1) - 1)
    def _():
        o12 = acc_scr[...] / l_scr[...]
        o = o12[:ATTN_HEADS] - _lam(lam_ref, lam_init) * o12[ATTN_HEADS:]
        o_ref[...] = _subln(o, g_ref[...], lam_init).astype(BF16)


def _attn_sample(page_table, qd, k_new, v_new, lam_vecs, g_subln, ck, cv, lam_init):
    bd, n_pages = page_table.shape
    n = PAGES_PER_STEP
    steps = n_pages // n

    def page_spec(p):
        return pl.BlockSpec((None, PAGE_ROWS, HEAD_V), lambda b, t, pt: (pt[b, t * n + p], 0, 0))

    rowb = lambda b, t, pt: (b, 0, 0)
    grid_spec = pltpu.PrefetchScalarGridSpec(
        num_scalar_prefetch=1,
        grid=(bd, steps),
        in_specs=[pl.BlockSpec((None, ROWS, HEAD_V), rowb),
                  pl.BlockSpec((None, ROWS, HEAD_V), rowb),
                  pl.BlockSpec((None, ROWS, HEAD_V), rowb),
                  pl.BlockSpec((8, QK_DIM), lambda b, t, pt: (0, 0)),
                  pl.BlockSpec((1, HEAD_V), lambda b, t, pt: (0, 0))]
                 + [page_spec(p) for p in range(n)] + [page_spec(p) for p in range(n)],
        out_specs=pl.BlockSpec((None, ATTN_HEADS, HEAD_V), rowb),
        scratch_shapes=[pltpu.VMEM((ROWS, 1), F32), pltpu.VMEM((ROWS, 1), F32),
                        pltpu.VMEM((ROWS, HEAD_V), F32)],
    )
    return pl.pallas_call(
        functools.partial(_attn_sample_kernel, lam_init),
        grid_spec=grid_spec,
        out_shape=jax.ShapeDtypeStruct((bd, ATTN_HEADS, HEAD_V), BF16),
        compiler_params=_cparams("parallel", "arbitrary"),
        name="attn_sample",
    )(page_table, qd, k_new, v_new, lam_vecs, g_subln, *([ck] * n), *([cv] * n))


def _merge_kernel(o_ref, c_ref, ga_ref, gc_ref, x_ref, wa_ref, wc_ref, wo_ref, g_ref, x1_ref, h2_ref):
    a_out = jnp.dot(o_ref[...], wa_ref[...], preferred_element_type=F32)
    c_out = jnp.dot(c_ref[...], wc_ref[...], preferred_element_type=F32)
    merged = ga_ref[...].astype(F32) * a_out + gc_ref[...].astype(F32) * c_out
    x1 = x_ref[...] + jnp.dot(merged.astype(BF16), wo_ref[...], preferred_element_type=F32)
    x1_ref[...] = x1
    h2_ref[...] = _rms(x1, g_ref[...]).astype(BF16)


def _merge(o, c, ga, gc, x, wa, wc, wo, g_ffn, tm):
    m = x.shape[0]
    row = lambda i: (i, 0)
    return pl.pallas_call(
        _merge_kernel,
        grid=(m // tm,),
        in_specs=[pl.BlockSpec((tm, ATTN_WIDTH), row), pl.BlockSpec((tm, CONV_DIM), row),
                  pl.BlockSpec((tm, D_MODEL), row), pl.BlockSpec((tm, D_MODEL), row),
                  pl.BlockSpec((tm, D_MODEL), row),
                  _const_spec((ATTN_WIDTH, D_MODEL)), _const_spec((CONV_DIM, D_MODEL)),
                  _const_spec((D_MODEL, D_MODEL)), _const_spec((1, D_MODEL))],
        out_specs=[pl.BlockSpec((tm, D_MODEL), row), pl.BlockSpec((tm, D_MODEL), row)],
        out_shape=[jax.ShapeDtypeStruct((m, D_MODEL), F32), jax.ShapeDtypeStruct((m, D_MODEL), BF16)],
        compiler_params=_cparams("parallel"),
        name="merge_out",
    )(o, c, ga, gc, x, wa, wc, wo, g_ffn)


FF_BLOCK = 512


def _ffn_kernel(h_ref, x1_ref, wg_ref, wu_ref, wd_ref, gf_ref, y_ref, acc_scr):
    j = pl.program_id(1)

    @pl.when(j == 0)
    def _():
        acc_scr[...] = jnp.zeros(acc_scr.shape, F32)

    h = h_ref[...]
    g = jnp.dot(h, wg_ref[...], preferred_element_type=F32)
    u = jnp.dot(h, wu_ref[...], preferred_element_type=F32)
    act = (g * jax.nn.sigmoid(g) * u).astype(BF16)
    acc_scr[...] += jnp.dot(act, wd_ref[...], preferred_element_type=F32)

    @pl.when(j == pl.num_programs(1) - 1)
    def _():
        y_ref[...] = _rms(x1_ref[...] + acc_scr[...], gf_ref[...])


def _ffn(h2, x1, w_gu, w_down, g_final, tm):
    m = h2.shape[0]
    nj = D_FF // FF_BLOCK
    row = lambda i, j: (i, 0)
    return pl.pallas_call(
        _ffn_kernel,
        grid=(m // tm, nj),
        in_specs=[pl.BlockSpec((tm, D_MODEL), row), pl.BlockSpec((tm, D_MODEL), row),
                  pl.BlockSpec((D_MODEL, FF_BLOCK), lambda i, j: (0, j)),
                  pl.BlockSpec((D_MODEL, FF_BLOCK), lambda i, j: (0, nj + j)),
                  pl.BlockSpec((FF_BLOCK, D_MODEL), lambda i, j: (j, 0)),
                  pl.BlockSpec((1, D_MODEL), lambda i, j: (0, 0))],
        out_specs=pl.BlockSpec((tm, D_MODEL), row),
        out_shape=jax.ShapeDtypeStruct((m, D_MODEL), F32),
        scratch_shapes=[pltpu.VMEM((tm, D_MODEL), F32)],
        compiler_params=_cparams("parallel", "arbitrary"),
        name="ffn",
    )(h2, x1, w_gu, w_gu, w_down, g_final)


def _rope_tables(pos):
    half = QK_DIM // 2
    inv = ROPE_THETA ** (-jnp.arange(half, dtype=F32) / half)
    ang = pos.astype(F32)[:, None] * inv[None, :]
    cos = jnp.tile(jnp.cos(ang), (1, LANES // half))
    sin = jnp.tile(jnp.sin(ang), (1, LANES // half))
    first_half = (jnp.arange(LANES) & half) == 0
    return cos, jnp.where(first_half[None, :], -sin, sin)


def kernel(x_prompt, x_sample, cache_k, cache_v, state_conv, page_table, w_in, w_conv, lambda_q1, lambda_k1, lambda_q2, lambda_k2, g_subln, w_branch_attn, w_branch_conv, w_out, g_mix, g_ffn, w_gate_up, w_down, g_final):
    bp, sp, _ = x_prompt.shape
    bd, sd, _ = x_sample.shape
    assert sd == 1 and w_in.shape[0] == 1
    n_pool = cache_k.shape[1]
    past = page_table.shape[1] * PAGE_SIZE
    lam_init = 0.8 - 0.6 * math.exp(-0.3 * 0)

    w_qkv = w_cv = w_gt = w_in[0].astype(BF16)
    wa = w_branch_attn[0].astype(BF16)
    wc = w_branch_conv[0].astype(BF16)
    wo = w_out[0].astype(BF16)
    w_gu = w_gate_up[0].astype(BF16)
    w_dn = w_down[0].astype(BF16)
    g_mix2 = g_mix[0].reshape(1, D_MODEL)
    g_ffn2 = g_ffn[0].reshape(1, D_MODEL)
    g_fin2 = g_final.reshape(1, D_MODEL)
    g_sub2 = g_subln[0].reshape(1, HEAD_V)
    lam_vecs = jnp.concatenate([lambda_q1, lambda_k1, lambda_q2, lambda_k2, jnp.zeros((4, QK_DIM), F32)], axis=0)
    wcv3 = w_conv[0]

    mp = bp * sp
    xp = x_prompt.reshape(mp, D_MODEL)
    cos_p, sin_p = _rope_tables(jnp.arange(sp, dtype=jnp.int32))
    tm = 512
    hp = _norm(xp, g_mix2, tm)
    qq, k_p, kb, v_p, vb = _qkv(hp, w_qkv, cos_p, sin_p, tm)
    c_p, st_p = _conv_prompt(hp, w_cv, wcv3, tm, sp)
    ga_p, gc_p = _gates(hp, w_gt, tm)
    o_p = _attn_prompt(qq, kb, vb, lam_vecs, g_sub2, lam_init, bp, sp, 1024, 512)
    x1_p, h2_p = _merge(o_p, c_p, ga_p, gc_p, xp, wa, wc, wo, g_ffn2, 256)
    y_p = _ffn(h2_p, x1_p, w_gu, w_dn, g_fin2, tm)

    xs = x_sample.reshape(bd, D_MODEL)
    cos_s, sin_s = _rope_tables(jnp.full((bd,), past, dtype=jnp.int32))
    hs = _norm(xs, g_mix2, bd)
    qq_s, k_s, _, v_s, _ = _qkv(hs, w_qkv, cos_s, sin_s, bd)
    c_s, st_s = _conv_sample(hs, w_cv, wcv3, state_conv[0].reshape(bd, 2 * CONV_DIM))
    ga_s, gc_s = _gates(hs, w_gt, bd)
    qd = jnp.transpose(qq_s.reshape(2, bd, ATTN_HEADS, HEAD_V), (1, 0, 2, 3)).reshape(bd, ROWS, HEAD_V)
    k_new = jnp.tile(k_s.reshape(bd, ATTN_HEADS, HEAD_V), (1, 2, 1))
    v_new = jnp.tile(v_s.reshape(bd, ATTN_HEADS, HEAD_V), (1, 2, 1))
    o_s = _attn_sample(page_table, qd, k_new, v_new, lam_vecs, g_sub2,
                       cache_k.reshape(n_pool, PAGE_ROWS, HEAD_V),
                       cache_v.reshape(n_pool, PAGE_ROWS, HEAD_V), lam_init)
    x1_s, h2_s = _merge(o_s.reshape(bd, ATTN_WIDTH), c_s, ga_s, gc_s, xs, wa, wc, wo, g_ffn2, bd)
    y_s = _ffn(h2_s, x1_s, w_gu, w_dn, g_fin2, bd)

    return (y_p.reshape(bp, sp, D_MODEL),
            y_s.reshape(bd, sd, D_MODEL),
            k_p.reshape(1, bp, sp, ATTN_HEADS, HEAD_V),
            v_p.reshape(1, bp, sp, ATTN_HEADS, HEAD_V),
            st_p[:, HALO - 2:, :].reshape(1, bp, CONV_WIDTH - 1, CONV_DIM),
            k_s.reshape(1, bd, sd, ATTN_HEADS, HEAD_V),
            v_s.reshape(1, bd, sd, ATTN_HEADS, HEAD_V),
            st_s.reshape(1, bd, CONV_WIDTH - 1, CONV_DIM))
```

```python
import functools
import math

import jax
import jax.numpy as jnp
from jax import lax
from jax.experimental import pallas as pl
from jax.experimental.pallas import tpu as pltpu

F32 = jnp.float32
BF16 = jnp.bfloat16

D_MODEL = 2048
ATTN_HEADS = 8
QK_DIM = 64
HEAD_V = 128
ATTN_WIDTH = ATTN_HEADS * HEAD_V
CONV_DIM = D_MODEL // 2
CONV_WIDTH = 3
D_FF = 5632
ROPE_THETA = 10000.0
PAGE_SIZE = 128
EPS = 1e-5
SCALE = QK_DIM ** -0.5
Q_SCALE = SCALE * math.log2(math.e)
NEG_INF = -1e30
LANES = 128

VMEM_LIMIT = 56 * 1024 * 1024


def _cparams(*sem):
    return pltpu.CompilerParams(dimension_semantics=sem, vmem_limit_bytes=VMEM_LIMIT)


def _const_spec(shape):
    nd = len(shape)
    return pl.BlockSpec(shape, lambda *_: (0,) * nd, pipeline_mode=pl.Buffered(1))


def _col_block_spec(rows, width, block):
    return pl.BlockSpec((rows, width), lambda *_: (0, block), pipeline_mode=pl.Buffered(1))


QKV_BLOCK = (3 * ATTN_WIDTH, 0)
CONV_BLOCK = (3 * CONV_DIM, 1)
GATE_A_BLOCK = (D_MODEL, 3)
GATE_C_BLOCK = (D_MODEL, 4)
assert QKV_BLOCK[0] == CONV_BLOCK[0] and GATE_A_BLOCK[1] * D_MODEL == QKV_BLOCK[0] + CONV_BLOCK[0]


def _rms(x, g):
    return x * lax.rsqrt(jnp.mean(x * x, axis=-1, keepdims=True) + EPS) * g


def _norm_kernel(x_ref, g_ref, h_ref):
    h_ref[...] = _rms(x_ref[...], g_ref[...]).astype(BF16)


def _norm(x, g, tm):
    m = x.shape[0]
    return pl.pallas_call(
        _norm_kernel,
        grid=(m // tm,),
        in_specs=[pl.BlockSpec((tm, D_MODEL), lambda i: (i, 0)), _const_spec((1, D_MODEL))],
        out_specs=pl.BlockSpec((tm, D_MODEL), lambda i: (i, 0)),
        out_shape=jax.ShapeDtypeStruct((m, D_MODEL), BF16),
        compiler_params=_cparams("parallel"),
        name="norm_in",
    )(x, g)


def _rope(z, cos, sin_signed, first_half):
    partner = jnp.where(first_half, pltpu.roll(z, 96, 1), pltpu.roll(z, 32, 1))
    return z * cos + partner * sin_signed


def _qkv_kernel(h_ref, w_ref, cos_ref, sin_ref, qq_ref, k_ref, kb_ref, v_ref, vb_ref):
    z = jnp.dot(h_ref[...], w_ref[...], preferred_element_type=F32)
    cos = cos_ref[...]
    sin = sin_ref[...]
    lane = lax.broadcasted_iota(jnp.int32, cos.shape, 1)
    first_half = (lane & 32) == 0
    map1 = lane < QK_DIM
    for hh in range(ATTN_HEADS):
        lo = hh * HEAD_V
        q = _rope(z[:, lo:lo + HEAD_V], cos, sin, first_half) * Q_SCALE
        qq_ref[0, :, lo:lo + HEAD_V] = jnp.where(map1, q, 0.0).astype(BF16)
        qq_ref[1, :, lo:lo + HEAD_V] = jnp.where(map1, 0.0, q).astype(BF16)
        k = _rope(z[:, ATTN_WIDTH + lo:ATTN_WIDTH + lo + HEAD_V], cos, sin, first_half)
        k_ref[:, lo:lo + HEAD_V] = k
        kb_ref[:, lo:lo + HEAD_V] = k.astype(BF16)
    v = z[:, 2 * ATTN_WIDTH:]
    v_ref[...] = v
    vb_ref[...] = v.astype(BF16)


def _qkv(h, w, cos, sin_signed, tm):
    m = h.shape[0]
    nt = cos.shape[0] // tm
    row = lambda i: (i, 0)
    return pl.pallas_call(
        _qkv_kernel,
        grid=(m // tm,),
        in_specs=[pl.BlockSpec((tm, D_MODEL), row),
                  _col_block_spec(D_MODEL, *QKV_BLOCK),
                  pl.BlockSpec((tm, LANES), lambda i: (i % nt, 0)),
                  pl.BlockSpec((tm, LANES), lambda i: (i % nt, 0))],
        out_specs=[pl.BlockSpec((2, tm, ATTN_WIDTH), lambda i: (0, i, 0)),
                   pl.BlockSpec((tm, ATTN_WIDTH), row), pl.BlockSpec((tm, ATTN_WIDTH), row),
                   pl.BlockSpec((tm, ATTN_WIDTH), row), pl.BlockSpec((tm, ATTN_WIDTH), row)],
        out_shape=[jax.ShapeDtypeStruct((2, m, ATTN_WIDTH), BF16),
                   jax.ShapeDtypeStruct((m, ATTN_WIDTH), F32), jax.ShapeDtypeStruct((m, ATTN_WIDTH), BF16),
                   jax.ShapeDtypeStruct((m, ATTN_WIDTH), F32), jax.ShapeDtypeStruct((m, ATTN_WIDTH), BF16)],
        compiler_params=_cparams("parallel"),
        name="qkv_rope",
    )(h, w, cos, sin_signed)


HALO = 8


def _conv_prompt_kernel(tiles_per_seq, h_ref, w_ref, wc_ref, c_ref, st_ref, u_scr):
    i = pl.program_id(0)
    tm = h_ref.shape[0]
    z = jnp.dot(h_ref[...], w_ref[...], preferred_element_type=F32)
    u = z[:, CONV_DIM:2 * CONV_DIM] * z[:, 2 * CONV_DIM:]

    @pl.when(i % tiles_per_seq == 0)
    def _():
        u_scr[0:HALO, :] = jnp.zeros((HALO, CONV_DIM), F32)

    @pl.when(i % tiles_per_seq != 0)
    def _():
        u_scr[0:HALO, :] = u_scr[tm:tm + HALO, :]

    u_scr[HALO:HALO + tm, :] = u
    wc = wc_ref[...]
    y = (wc[0:1, :] * u_scr[HALO - 2:HALO - 2 + tm, :]
         + wc[1:2, :] * u_scr[HALO - 1:HALO - 1 + tm, :]
         + wc[2:3, :] * u)
    c_ref[...] = (z[:, :CONV_DIM] * y).astype(BF16)
    st_ref[...] = u[tm - HALO:, :]


def _conv_prompt(h, w, w_conv, tm, seq):
    m = h.shape[0]
    tiles_per_seq = seq // tm
    return pl.pallas_call(
        functools.partial(_conv_prompt_kernel, tiles_per_seq),
        grid=(m // tm,),
        in_specs=[pl.BlockSpec((tm, D_MODEL), lambda i: (i, 0)),
                  _col_block_spec(D_MODEL, *CONV_BLOCK),
                  _const_spec((CONV_WIDTH, CONV_DIM))],
        out_specs=[pl.BlockSpec((tm, CONV_DIM), lambda i: (i, 0)),
                   pl.BlockSpec((None, HALO, CONV_DIM), lambda i: (i // tiles_per_seq, 0, 0))],
        out_shape=[jax.ShapeDtypeStruct((m, CONV_DIM), BF16),
                   jax.ShapeDtypeStruct((m // seq, HALO, CONV_DIM), F32)],
        scratch_shapes=[pltpu.VMEM((tm + HALO, CONV_DIM), F32)],
        compiler_params=_cparams("arbitrary"),
        name="conv_prompt",
    )(h, w, w_conv)


def _conv_sample_kernel(h_ref, w_ref, wc_ref, st_ref, c_ref, nst_ref):
    z = jnp.dot(h_ref[...], w_ref[...], preferred_element_type=F32)
    u = z[:, CONV_DIM:2 * CONV_DIM] * z[:, 2 * CONV_DIM:]
    wc = wc_ref[...]
    s0 = st_ref[:, :CONV_DIM]
    s1 = st_ref[:, CONV_DIM:]
    y = wc[0:1, :] * s0 + wc[1:2, :] * s1 + wc[2:3, :] * u
    c_ref[...] = (z[:, :CONV_DIM] * y).astype(BF16)
    nst_ref[:, :CONV_DIM] = s1
    nst_ref[:, CONV_DIM:] = u


def _conv_sample(h, w, w_conv, state2d):
    m = h.shape[0]
    return pl.pallas_call(
        _conv_sample_kernel,
        grid=(1,),
        in_specs=[_const_spec((m, D_MODEL)), _col_block_spec(D_MODEL, *CONV_BLOCK),
                  _const_spec((CONV_WIDTH, CONV_DIM)), _const_spec((m, 2 * CONV_DIM))],
        out_specs=[pl.BlockSpec((m, CONV_DIM), lambda i: (0, 0)),
                   pl.BlockSpec((m, 2 * CONV_DIM), lambda i: (0, 0))],
        out_shape=[jax.ShapeDtypeStruct((m, CONV_DIM), BF16),
                   jax.ShapeDtypeStruct((m, 2 * CONV_DIM), F32)],
        compiler_params=_cparams("arbitrary"),
        name="conv_sample",
    )(h, w, w_conv, state2d)


def _gate_kernel(h_ref, wa_ref, wc_ref, ga_ref, gc_ref):
    h = h_ref[...]
    ga_ref[...] = jax.nn.sigmoid(jnp.dot(h, wa_ref[...], preferred_element_type=F32)).astype(BF16)
    gc_ref[...] = jax.nn.sigmoid(jnp.dot(h, wc_ref[...], preferred_element_type=F32)).astype(BF16)


def _gates(h, w, tm):
    m = h.shape[0]
    row = lambda i: (i, 0)
    return pl.pallas_call(
        _gate_kernel,
        grid=(m // tm,),
        in_specs=[pl.BlockSpec((tm, D_MODEL), row),
                  _col_block_spec(D_MODEL, *GATE_A_BLOCK), _col_block_spec(D_MODEL, *GATE_C_BLOCK)],
        out_specs=[pl.BlockSpec((tm, D_MODEL), row), pl.BlockSpec((tm, D_MODEL), row)],
        out_shape=[jax.ShapeDtypeStruct((m, D_MODEL), BF16), jax.ShapeDtypeStruct((m, D_MODEL), BF16)],
        compiler_params=_cparams("parallel"),
        name="gates",
    )(h, w, w)


def _lam(lam_ref, lam_init):
    lv = lam_ref[...]
    a = jnp.sum(lv[0:1, :] * lv[1:2, :], axis=-1, keepdims=True)
    b = jnp.sum(lv[2:3, :] * lv[3:4, :], axis=-1, keepdims=True)
    return jnp.exp(a) - jnp.exp(b) + lam_init


def _subln(o, g, lam_init):
    return o * lax.rsqrt(jnp.mean(o * o, axis=-1, keepdims=True) + EPS) * g * (1.0 - lam_init)


ROW_BLOCK = 128


def _attn_tile(diag, qq_ref, k_ref, v_ref, m_scr, acc_scr):
    tq = qq_ref.shape[1]
    tk = k_ref.shape[0]
    k = k_ref[...]
    v_ext = jnp.concatenate([v_ref[...], jnp.ones((tk, HEAD_V), BF16)], axis=1)
    for mp in range(2):
        for r0 in range(0, tq, ROW_BLOCK):
            c0 = 0 if diag is None else diag * tk
            if r0 + ROW_BLOCK - 1 < c0:
                continue
            rows = pl.ds(mp * tq + r0, ROW_BLOCK)
            s = lax.dot_general(qq_ref[mp, r0:r0 + ROW_BLOCK, :], k, (((1,), (1,)), ((), ())),
                                preferred_element_type=F32)
            if diag is not None and r0 < c0 + tk - 1:
                qpos = r0 + lax.broadcasted_iota(jnp.int32, s.shape, 0)
                kpos = c0 + lax.broadcasted_iota(jnp.int32, s.shape, 1)
                s = jnp.where(kpos <= qpos, s, NEG_INF)
            m_prev = m_scr[rows, :]
            m_new = jnp.maximum(m_prev, jnp.max(s, axis=1, keepdims=True))
            p = jnp.exp2(s - jnp.tile(m_new, (1, tk // LANES)))
            alpha = jnp.exp2(m_prev - m_new)
            pv = jnp.dot(p.astype(BF16), v_ext, preferred_element_type=F32)
            acc_scr[rows, :] = jnp.tile(alpha, (1, 2)) * acc_scr[rows, :] + pv
            m_scr[rows, :] = m_new


def _attn_prompt_step(lam_init, i, j, qq_ref, k_ref, v_ref, lam_ref, g_ref, o_ref, m_scr, acc_scr):
    tq = qq_ref.shape[1]
    ratio = tq // k_ref.shape[0]

    @pl.when(j == 0)
    def _():
        m_scr[...] = jnp.full(m_scr.shape, NEG_INF, F32)
        acc_scr[...] = jnp.zeros(acc_scr.shape, F32)

    @pl.when(j < ratio * i)
    def _():
        _attn_tile(None, qq_ref, k_ref, v_ref, m_scr, acc_scr)

    for d in range(ratio):
        @pl.when(j == ratio * i + d)
        def _(d=d):
            _attn_tile(d, qq_ref, k_ref, v_ref, m_scr, acc_scr)

    @pl.when(j == ratio * i + ratio - 1)
    def _():
        acc = acc_scr[...]
        o12 = acc[:, :HEAD_V] / acc[:, HEAD_V:]
        o = o12[:tq] - _lam(lam_ref, lam_init) * o12[tq:]
        o_ref[...] = _subln(o, g_ref[...], lam_init).astype(BF16)


PAGES_PER_STEP = 16
ROWS = 2 * ATTN_HEADS
PAGE_ROWS = PAGE_SIZE * ATTN_HEADS


def _attn_sample_step(lam_init, first, last, qd_ref, kn_ref, vn_ref, lam_ref, g_ref, k_refs, v_refs, o_ref,
                      m_scr, l_scr, acc_scr):
    n = len(k_refs)
    qd = qd_ref[...]

    @pl.when(first)
    def _():
        m_scr[...] = jnp.sum(qd.astype(F32) * kn_ref[...], axis=-1, keepdims=True)
        l_scr[...] = jnp.ones(l_scr.shape, F32)
        acc_scr[...] = vn_ref[...]

    s = jnp.concatenate(
        [lax.dot_general(qd, k_refs[p][...].astype(BF16), (((1,), (1,)), ((), ())),
                         preferred_element_type=F32) for p in range(n)], axis=1)
    col = lax.broadcasted_iota(jnp.int32, s.shape, 1)
    row = lax.broadcasted_iota(jnp.int32, s.shape, 0)
    s = jnp.where((col % ATTN_HEADS) == (row % ATTN_HEADS), s, NEG_INF)
    m_prev = m_scr[...]
    m_new = jnp.maximum(m_prev, jnp.max(s, axis=-1, keepdims=True))
    alpha = jnp.exp2(m_prev - m_new)
    p_all = jnp.exp2(s - m_new)
    l_scr[...] = alpha * l_scr[...] + jnp.sum(p_all, axis=-1, keepdims=True)
    pb = p_all.astype(BF16)
    pv = jnp.dot(pb[:, 0:PAGE_ROWS], v_refs[0][...].astype(BF16), preferred_element_type=F32)
    for p in range(1, n):
        pv += jnp.dot(pb[:, p * PAGE_ROWS:(p + 1) * PAGE_ROWS], v_refs[p][...].astype(BF16),
                      preferred_element_type=F32)
    acc_scr[...] = alpha * acc_scr[...] + pv
    m_scr[...] = m_new

    @pl.when(last)
    def _():
        o12 = acc_scr[...] / l_scr[...]
        o = o12[:ATTN_HEADS] - _lam(lam_ref, lam_init) * o12[ATTN_HEADS:]
        o_ref[...] = _subln(o, g_ref[...], lam_init).astype(BF16)


def _attn_kernel(lam_init, n_chunks, chunks_per_seq, *refs):
    n = PAGES_PER_STEP
    it_ref, jt_ref = refs[:2]
    qq_ref, k_ref, v_ref, lam_ref, g_ref, qd_ref, kn_ref, vn_ref = refs[4:12]
    k_refs = refs[12:12 + n]
    v_refs = refs[12 + n:12 + 2 * n]
    o_ref, os_ref = refs[12 + 2 * n:14 + 2 * n]
    m_scr, acc_scr, dm_scr, dl_scr, dacc_scr = refs[14 + 2 * n:]

    step = pl.program_id(2)
    _attn_prompt_step(lam_init, it_ref[step], jt_ref[step], qq_ref, k_ref, v_ref, lam_ref, g_ref, o_ref,
                      m_scr, acc_scr)

    lin = (pl.program_id(0) * pl.num_programs(1) + pl.program_id(1)) * pl.num_programs(2) + step

    @pl.when(lin < n_chunks)
    def _():
        t = lin % chunks_per_seq
        _attn_sample_step(lam_init, t == 0, t == chunks_per_seq - 1, qd_ref, kn_ref, vn_ref, lam_ref, g_ref,
                          k_refs, v_refs, os_ref, dm_scr, dl_scr, dacc_scr)


def _attn(qq, kb, vb, page_table, qd, k_new, v_new, ck, cv, lam_vecs, g_subln, lam_init, batch, seq, tq, tk):
    m = kb.shape[0]
    nq = seq // tq
    nk = seq // tk
    ratio = tq // tk
    pairs = [(i, j) for i in range(nq) for j in range(ratio * (i + 1))]
    i_tab = jnp.array([p[0] for p in pairs], jnp.int32)
    j_tab = jnp.array([p[1] for p in pairs], jnp.int32)
    npairs = len(pairs)

    bd, n_pages = page_table.shape
    n = PAGES_PER_STEP
    chunks_per_seq = n_pages // n
    n_chunks = bd * chunks_per_seq
    assert n_pages % n == 0 and n_chunks <= batch * ATTN_HEADS * npairs

    steps = batch * ATTN_HEADS * npairs
    chunk_of_step = [min(t, n_chunks - 1) for t in range(steps)]
    pages = page_table.reshape(n_chunks, n)[jnp.array(chunk_of_step, jnp.int32)].reshape(steps * n)
    seq_tab = jnp.array([c // chunks_per_seq for c in chunk_of_step], jnp.int32)

    def lin(b, h, s):
        return (b * ATTN_HEADS + h) * npairs + s

    def page_spec(p):
        return pl.BlockSpec((None, PAGE_ROWS, HEAD_V),
                            lambda b, h, s, it, jt, pg, sq: (pg[lin(b, h, s) * n + p], 0, 0))

    seq_row = lambda b, h, s, it, jt, pg, sq: (sq[lin(b, h, s)], 0, 0)
    fixed = lambda b, h, s, it, jt, pg, sq: (0, 0)
    grid_spec = pltpu.PrefetchScalarGridSpec(
        num_scalar_prefetch=4,
        grid=(batch, ATTN_HEADS, npairs),
        in_specs=[pl.BlockSpec((2, tq, HEAD_V), lambda b, h, s, it, jt, pg, sq: (0, b * nq + it[s], h)),
                  pl.BlockSpec((tk, HEAD_V), lambda b, h, s, it, jt, pg, sq: (b * nk + jt[s], h)),
                  pl.BlockSpec((tk, HEAD_V), lambda b, h, s, it, jt, pg, sq: (b * nk + jt[s], h)),
                  pl.BlockSpec((8, QK_DIM), fixed),
                  pl.BlockSpec((1, HEAD_V), fixed),
                  pl.BlockSpec((None, ROWS, HEAD_V), seq_row),
                  pl.BlockSpec((None, ROWS, HEAD_V), seq_row),
                  pl.BlockSpec((None, ROWS, HEAD_V), seq_row)]
                 + [page_spec(p) for p in range(n)] + [page_spec(p) for p in range(n)],
        out_specs=[pl.BlockSpec((tq, HEAD_V), lambda b, h, s, it, jt, pg, sq: (b * nq + it[s], h)),
                   pl.BlockSpec((None, ATTN_HEADS, HEAD_V), seq_row)],
        scratch_shapes=[pltpu.VMEM((2 * tq, LANES), F32), pltpu.VMEM((2 * tq, 2 * HEAD_V), F32),
                        pltpu.VMEM((ROWS, 1), F32), pltpu.VMEM((ROWS, 1), F32), pltpu.VMEM((ROWS, HEAD_V), F32)],
    )
    return pl.pallas_call(
        functools.partial(_attn_kernel, lam_init, n_chunks, chunks_per_seq),
        grid_spec=grid_spec,
        out_shape=[jax.ShapeDtypeStruct((m, ATTN_WIDTH), BF16),
                   jax.ShapeDtypeStruct((bd, ATTN_HEADS, HEAD_V), BF16)],
        compiler_params=_cparams("arbitrary", "arbitrary", "arbitrary"),
        name="attn",
    )(i_tab, j_tab, pages, seq_tab, qq, kb, vb, lam_vecs, g_subln, qd, k_new, v_new, *([ck] * n), *([cv] * n))


def _merge_kernel(o_ref, c_ref, ga_ref, gc_ref, x_ref, wa_ref, wc_ref, wo_ref, g_ref, x1_ref, h2_ref):
    a_out = jnp.dot(o_ref[...], wa_ref[...], preferred_element_type=F32)
    c_out = jnp.dot(c_ref[...], wc_ref[...], preferred_element_type=F32)
    merged = ga_ref[...].astype(F32) * a_out + gc_ref[...].astype(F32) * c_out
    x1 = x_ref[...] + jnp.dot(merged.astype(BF16), wo_ref[...], preferred_element_type=F32)
    x1_ref[...] = x1
    h2_ref[...] = _rms(x1, g_ref[...]).astype(BF16)


def _merge(o, c, ga, gc, x, wa, wc, wo, g_ffn, tm):
    m = x.shape[0]
    row = lambda i: (i, 0)
    return pl.pallas_call(
        _merge_kernel,
        grid=(m // tm,),
        in_specs=[pl.BlockSpec((tm, ATTN_WIDTH), row), pl.BlockSpec((tm, CONV_DIM), row),
                  pl.BlockSpec((tm, D_MODEL), row), pl.BlockSpec((tm, D_MODEL), row),
                  pl.BlockSpec((tm, D_MODEL), row),
                  _const_spec((ATTN_WIDTH, D_MODEL)), _const_spec((CONV_DIM, D_MODEL)),
                  _const_spec((D_MODEL, D_MODEL)), _const_spec((1, D_MODEL))],
        out_specs=[pl.BlockSpec((tm, D_MODEL), row), pl.BlockSpec((tm, D_MODEL), row)],
        out_shape=[jax.ShapeDtypeStruct((m, D_MODEL), F32), jax.ShapeDtypeStruct((m, D_MODEL), BF16)],
        compiler_params=_cparams("parallel"),
        name="merge_out",
    )(o, c, ga, gc, x, wa, wc, wo, g_ffn)


FF_BLOCK = 512


def _ffn_kernel(h_ref, x1_ref, wg_ref, wu_ref, wd_ref, gf_ref, y_ref, acc_scr):
    j = pl.program_id(1)

    @pl.when(j == 0)
    def _():
        acc_scr[...] = jnp.zeros(acc_scr.shape, F32)

    h = h_ref[...]
    g = jnp.dot(h, wg_ref[...], preferred_element_type=F32)
    u = jnp.dot(h, wu_ref[...], preferred_element_type=F32)
    act = (g * jax.nn.sigmoid(g) * u).astype(BF16)
    acc_scr[...] += jnp.dot(act, wd_ref[...], preferred_element_type=F32)

    @pl.when(j == pl.num_programs(1) - 1)
    def _():
        y_ref[...] = _rms(x1_ref[...] + acc_scr[...], gf_ref[...])


def _ffn(h2, x1, w_gu, w_down, g_final, tm):
    m = h2.shape[0]
    nj = D_FF // FF_BLOCK
    row = lambda i, j: (i, 0)
    return pl.pallas_call(
        _ffn_kernel,
        grid=(m // tm, nj),
        in_specs=[pl.BlockSpec((tm, D_MODEL), row), pl.BlockSpec((tm, D_MODEL), row),
                  pl.BlockSpec((D_MODEL, FF_BLOCK), lambda i, j: (0, j)),
                  pl.BlockSpec((D_MODEL, FF_BLOCK), lambda i, j: (0, nj + j)),
                  pl.BlockSpec((FF_BLOCK, D_MODEL), lambda i, j: (j, 0)),
                  pl.BlockSpec((1, D_MODEL), lambda i, j: (0, 0))],
        out_specs=pl.BlockSpec((tm, D_MODEL), row),
        out_shape=jax.ShapeDtypeStruct((m, D_MODEL), F32),
        scratch_shapes=[pltpu.VMEM((tm, D_MODEL), F32)],
        compiler_params=_cparams("parallel", "arbitrary"),
        name="ffn",
    )(h2, x1, w_gu, w_gu, w_down, g_final)


def _rope_tables(pos):
    half = QK_DIM // 2
    inv = ROPE_THETA ** (-jnp.arange(half, dtype=F32) / half)
    ang = pos.astype(F32)[:, None] * inv[None, :]
    cos = jnp.tile(jnp.cos(ang), (1, LANES // half))
    sin = jnp.tile(jnp.sin(ang), (1, LANES // half))
    first_half = (jnp.arange(LANES) & half) == 0
    return cos, jnp.where(first_half[None, :], -sin, sin)


def kernel(x_prompt, x_sample, cache_k, cache_v, state_conv, page_table, w_in, w_conv, lambda_q1, lambda_k1, lambda_q2, lambda_k2, g_subln, w_branch_attn, w_branch_conv, w_out, g_mix, g_ffn, w_gate_up, w_down, g_final):
    bp, sp, _ = x_prompt.shape
    bd, sd, _ = x_sample.shape
    assert sd == 1 and w_in.shape[0] == 1
    n_pool = cache_k.shape[1]
    past = page_table.shape[1] * PAGE_SIZE
    lam_init = 0.8 - 0.6 * math.exp(-0.3 * 0)

    w_qkv = w_cv = w_gt = w_in[0].astype(BF16)
    wa = w_branch_attn[0].astype(BF16)
    wc = w_branch_conv[0].astype(BF16)
    wo = w_out[0].astype(BF16)
    w_gu = w_gate_up[0].astype(BF16)
    w_dn = w_down[0].astype(BF16)
    g_mix2 = g_mix[0].reshape(1, D_MODEL)
    g_ffn2 = g_ffn[0].reshape(1, D_MODEL)
    g_fin2 = g_final.reshape(1, D_MODEL)
    g_sub2 = g_subln[0].reshape(1, HEAD_V)
    lam_vecs = jnp.concatenate([lambda_q1, lambda_k1, lambda_q2, lambda_k2, jnp.zeros((4, QK_DIM), F32)], axis=0)
    wcv3 = w_conv[0]

    mp = bp * sp
    xp = x_prompt.reshape(mp, D_MODEL)
    cos_p, sin_p = _rope_tables(jnp.arange(sp, dtype=jnp.int32))
    tm = 512
    hp = _norm(xp, g_mix2, tm)
    qq, k_p, kb, v_p, vb = _qkv(hp, w_qkv, cos_p, sin_p, tm)
    c_p, st_p = _conv_prompt(hp, w_cv, wcv3, tm, sp)
    ga_p, gc_p = _gates(hp, w_gt, tm)

    xs = x_sample.reshape(bd, D_MODEL)
    cos_s, sin_s = _rope_tables(jnp.full((bd,), past, dtype=jnp.int32))
    hs = _norm(xs, g_mix2, bd)
    qq_s, k_s, _, v_s, _ = _qkv(hs, w_qkv, cos_s, sin_s, bd)
    c_s, st_s = _conv_sample(hs, w_cv, wcv3, state_conv[0].reshape(bd, 2 * CONV_DIM))
    ga_s, gc_s = _gates(hs, w_gt, bd)
    qd = jnp.transpose(qq_s.reshape(2, bd, ATTN_HEADS, HEAD_V), (1, 0, 2, 3)).reshape(bd, ROWS, HEAD_V)
    k_new = jnp.tile(k_s.reshape(bd, ATTN_HEADS, HEAD_V), (1, 2, 1))
    v_new = jnp.tile(v_s.reshape(bd, ATTN_HEADS, HEAD_V), (1, 2, 1))

    o_p, o_s = _attn(qq, kb, vb, page_table, qd, k_new, v_new,
                     cache_k.reshape(n_pool, PAGE_ROWS, HEAD_V), cache_v.reshape(n_pool, PAGE_ROWS, HEAD_V),
                     lam_vecs, g_sub2, lam_init, bp, sp, 1024, 512)
    x1_p, h2_p = _merge(o_p, c_p, ga_p, gc_p, xp, wa, wc, wo, g_ffn2, 256)
    y_p = _ffn(h2_p, x1_p, w_gu, w_dn, g_fin2, tm)
    x1_s, h2_s = _merge(o_s.reshape(bd, ATTN_WIDTH), c_s, ga_s, gc_s, xs, wa, wc, wo, g_ffn2, bd)
    y_s = _ffn(h2_s, x1_s, w_gu, w_dn, g_fin2, bd)

    return (y_p.reshape(bp, sp, D_MODEL),
            y_s.reshape(bd, sd, D_MODEL),
            k_p.reshape(1, bp, sp, ATTN_HEADS, HEAD_V),
            v_p.reshape(1, bp, sp, ATTN_HEADS, HEAD_V),
            st_p[:, HALO - 2:, :].reshape(1, bp, CONV_WIDTH - 1, CONV_DIM),
            k_s.reshape(1, bd, sd, ATTN_HEADS, HEAD_V),
            v_s.reshape(1, bd, sd, ATTN_HEADS, HEAD_V),
            st_s.reshape(1, bd, CONV_WIDTH - 1, CONV_DIM))
```

```python
import functools
import math

import jax
import jax.numpy as jnp
from jax import lax
from jax.experimental import pallas as pl
from jax.experimental.pallas import tpu as pltpu

F32 = jnp.float32
BF16 = jnp.bfloat16

D_MODEL = 2048
ATTN_HEADS = 8
QK_DIM = 64
HEAD_V = 128
ATTN_WIDTH = ATTN_HEADS * HEAD_V
CONV_DIM = D_MODEL // 2
CONV_WIDTH = 3
D_FF = 5632
ROPE_THETA = 10000.0
PAGE_SIZE = 128
EPS = 1e-5
SCALE = QK_DIM ** -0.5
Q_SCALE = SCALE * math.log2(math.e)
NEG_INF = -1e30
LANES = 128

VMEM_LIMIT = 56 * 1024 * 1024


def _cparams(*sem):
    return pltpu.CompilerParams(dimension_semantics=sem, vmem_limit_bytes=VMEM_LIMIT)


def _const_spec(shape):
    nd = len(shape)
    return pl.BlockSpec(shape, lambda *_: (0,) * nd, pipeline_mode=pl.Buffered(1))


def _col_block_spec(rows, width, block):
    return pl.BlockSpec((rows, width), lambda *_: (0, block), pipeline_mode=pl.Buffered(1))


QKV_BLOCK = (3 * ATTN_WIDTH, 0)
CONV_BLOCK = (3 * CONV_DIM, 1)
GATE_A_BLOCK = (D_MODEL, 3)
GATE_C_BLOCK = (D_MODEL, 4)
assert QKV_BLOCK[0] == CONV_BLOCK[0] and GATE_A_BLOCK[1] * D_MODEL == QKV_BLOCK[0] + CONV_BLOCK[0]


def _rms(x, g):
    return x * lax.rsqrt(jnp.mean(x * x, axis=-1, keepdims=True) + EPS) * g


def _rope(z, cos, sin_signed, first_half):
    partner = jnp.where(first_half, pltpu.roll(z, 96, 1), pltpu.roll(z, 32, 1))
    return z * cos + partner * sin_signed


def _qkv_kernel(x_ref, g_ref, w_ref, cos_ref, sin_ref, h_ref, qq_ref, k_ref, kb_ref, v_ref, vb_ref):
    h = _rms(x_ref[...], g_ref[...]).astype(BF16)
    h_ref[...] = h
    z = jnp.dot(h, w_ref[...], preferred_element_type=F32)
    cos = cos_ref[...]
    sin = sin_ref[...]
    lane = lax.broadcasted_iota(jnp.int32, cos.shape, 1)
    first_half = (lane & 32) == 0
    map1 = lane < QK_DIM
    for hh in range(ATTN_HEADS):
        lo = hh * HEAD_V
        q = _rope(z[:, lo:lo + HEAD_V], cos, sin, first_half) * Q_SCALE
        qq_ref[0, :, lo:lo + HEAD_V] = jnp.where(map1, q, 0.0).astype(BF16)
        qq_ref[1, :, lo:lo + HEAD_V] = jnp.where(map1, 0.0, q).astype(BF16)
        k = _rope(z[:, ATTN_WIDTH + lo:ATTN_WIDTH + lo + HEAD_V], cos, sin, first_half)
        k_ref[:, lo:lo + HEAD_V] = k
        kb_ref[:, lo:lo + HEAD_V] = k.astype(BF16)
    v = z[:, 2 * ATTN_WIDTH:]
    v_ref[...] = v
    vb_ref[...] = v.astype(BF16)


def _qkv(x, g, w, cos, sin_signed, tm):
    m = x.shape[0]
    nt = cos.shape[0] // tm
    row = lambda i: (i, 0)
    return pl.pallas_call(
        _qkv_kernel,
        grid=(m // tm,),
        in_specs=[pl.BlockSpec((tm, D_MODEL), row),
                  _const_spec((1, D_MODEL)),
                  _col_block_spec(D_MODEL, *QKV_BLOCK),
                  pl.BlockSpec((tm, LANES), lambda i: (i % nt, 0)),
                  pl.BlockSpec((tm, LANES), lambda i: (i % nt, 0))],
        out_specs=[pl.BlockSpec((tm, D_MODEL), row),
                   pl.BlockSpec((2, tm, ATTN_WIDTH), lambda i: (0, i, 0)),
                   pl.BlockSpec((tm, ATTN_WIDTH), row), pl.BlockSpec((tm, ATTN_WIDTH), row),
                   pl.BlockSpec((tm, ATTN_WIDTH), row), pl.BlockSpec((tm, ATTN_WIDTH), row)],
        out_shape=[jax.ShapeDtypeStruct((m, D_MODEL), BF16),
                   jax.ShapeDtypeStruct((2, m, ATTN_WIDTH), BF16),
                   jax.ShapeDtypeStruct((m, ATTN_WIDTH), F32), jax.ShapeDtypeStruct((m, ATTN_WIDTH), BF16),
                   jax.ShapeDtypeStruct((m, ATTN_WIDTH), F32), jax.ShapeDtypeStruct((m, ATTN_WIDTH), BF16)],
        compiler_params=_cparams("parallel"),
        name="qkv_rope",
    )(x, g, w, cos, sin_signed)


HALO = 8


def _conv_prompt_kernel(tiles_per_seq, h_ref, w_ref, wc_ref, c_ref, st_ref, u_scr):
    i = pl.program_id(0)
    tm = h_ref.shape[0]
    z = jnp.dot(h_ref[...], w_ref[...], preferred_element_type=F32)
    u = z[:, CONV_DIM:2 * CONV_DIM] * z[:, 2 * CONV_DIM:]

    @pl.when(i % tiles_per_seq == 0)
    def _():
        u_scr[0:HALO, :] = jnp.zeros((HALO, CONV_DIM), F32)

    @pl.when(i % tiles_per_seq != 0)
    def _():
        u_scr[0:HALO, :] = u_scr[tm:tm + HALO, :]

    u_scr[HALO:HALO + tm, :] = u
    wc = wc_ref[...]
    y = (wc[0:1, :] * u_scr[HALO - 2:HALO - 2 + tm, :]
         + wc[1:2, :] * u_scr[HALO - 1:HALO - 1 + tm, :]
         + wc[2:3, :] * u)
    c_ref[...] = (z[:, :CONV_DIM] * y).astype(BF16)
    st_ref[...] = u[tm - HALO:, :]


def _conv_prompt(h, w, w_conv, tm, seq):
    m = h.shape[0]
    tiles_per_seq = seq // tm
    return pl.pallas_call(
        functools.partial(_conv_prompt_kernel, tiles_per_seq),
        grid=(m // tm,),
        in_specs=[pl.BlockSpec((tm, D_MODEL), lambda i: (i, 0)),
                  _col_block_spec(D_MODEL, *CONV_BLOCK),
                  _const_spec((CONV_WIDTH, CONV_DIM))],
        out_specs=[pl.BlockSpec((tm, CONV_DIM), lambda i: (i, 0)),
                   pl.BlockSpec((None, HALO, CONV_DIM), lambda i: (i // tiles_per_seq, 0, 0))],
        out_shape=[jax.ShapeDtypeStruct((m, CONV_DIM), BF16),
                   jax.ShapeDtypeStruct((m // seq, HALO, CONV_DIM), F32)],
        scratch_shapes=[pltpu.VMEM((tm + HALO, CONV_DIM), F32)],
        compiler_params=_cparams("arbitrary"),
        name="conv_prompt",
    )(h, w, w_conv)


def _conv_sample_kernel(h_ref, w_ref, wc_ref, st_ref, c_ref, nst_ref):
    z = jnp.dot(h_ref[...], w_ref[...], preferred_element_type=F32)
    u = z[:, CONV_DIM:2 * CONV_DIM] * z[:, 2 * CONV_DIM:]
    wc = wc_ref[...]
    s0 = st_ref[:, :CONV_DIM]
    s1 = st_ref[:, CONV_DIM:]
    y = wc[0:1, :] * s0 + wc[1:2, :] * s1 + wc[2:3, :] * u
    c_ref[...] = (z[:, :CONV_DIM] * y).astype(BF16)
    nst_ref[:, :CONV_DIM] = s1
    nst_ref[:, CONV_DIM:] = u


def _conv_sample(h, w, w_conv, state2d):
    m = h.shape[0]
    return pl.pallas_call(
        _conv_sample_kernel,
        grid=(1,),
        in_specs=[_const_spec((m, D_MODEL)), _col_block_spec(D_MODEL, *CONV_BLOCK),
                  _const_spec((CONV_WIDTH, CONV_DIM)), _const_spec((m, 2 * CONV_DIM))],
        out_specs=[pl.BlockSpec((m, CONV_DIM), lambda i: (0, 0)),
                   pl.BlockSpec((m, 2 * CONV_DIM), lambda i: (0, 0))],
        out_shape=[jax.ShapeDtypeStruct((m, CONV_DIM), BF16),
                   jax.ShapeDtypeStruct((m, 2 * CONV_DIM), F32)],
        compiler_params=_cparams("arbitrary"),
        name="conv_sample",
    )(h, w, w_conv, state2d)


def _gate_kernel(h_ref, wa_ref, wc_ref, ga_ref, gc_ref):
    h = h_ref[...]
    ga_ref[...] = jax.nn.sigmoid(jnp.dot(h, wa_ref[...], preferred_element_type=F32)).astype(BF16)
    gc_ref[...] = jax.nn.sigmoid(jnp.dot(h, wc_ref[...], preferred_element_type=F32)).astype(BF16)


def _gates(h, w, tm):
    m = h.shape[0]
    row = lambda i: (i, 0)
    return pl.pallas_call(
        _gate_kernel,
        grid=(m // tm,),
        in_specs=[pl.BlockSpec((tm, D_MODEL), row),
                  _col_block_spec(D_MODEL, *GATE_A_BLOCK), _col_block_spec(D_MODEL, *GATE_C_BLOCK)],
        out_specs=[pl.BlockSpec((tm, D_MODEL), row), pl.BlockSpec((tm, D_MODEL), row)],
        out_shape=[jax.ShapeDtypeStruct((m, D_MODEL), BF16), jax.ShapeDtypeStruct((m, D_MODEL), BF16)],
        compiler_params=_cparams("parallel"),
        name="gates",
    )(h, w, w)


def _lam(lam_ref, lam_init):
    lv = lam_ref[...]
    a = jnp.sum(lv[0:1, :] * lv[1:2, :], axis=-1, keepdims=True)
    b = jnp.sum(lv[2:3, :] * lv[3:4, :], axis=-1, keepdims=True)
    return jnp.exp(a) - jnp.exp(b) + lam_init


def _subln(o, g, lam_init):
    return o * lax.rsqrt(jnp.mean(o * o, axis=-1, keepdims=True) + EPS) * g * (1.0 - lam_init)


ROW_BLOCK = 128


def _attn_tile(diag, qq_ref, k_ref, v_ref, m_scr, acc_scr):
    tq = qq_ref.shape[1]
    tk = k_ref.shape[0]
    k = k_ref[...]
    v_ext = jnp.concatenate([v_ref[...], jnp.ones((tk, HEAD_V), BF16)], axis=1)
    for mp in range(2):
        for r0 in range(0, tq, ROW_BLOCK):
            c0 = 0 if diag is None else diag * tk
            if r0 + ROW_BLOCK - 1 < c0:
                continue
            rows = pl.ds(mp * tq + r0, ROW_BLOCK)
            s = lax.dot_general(qq_ref[mp, r0:r0 + ROW_BLOCK, :], k, (((1,), (1,)), ((), ())),
                                preferred_element_type=F32)
            if diag is not None and r0 < c0 + tk - 1:
                qpos = r0 + lax.broadcasted_iota(jnp.int32, s.shape, 0)
                kpos = c0 + lax.broadcasted_iota(jnp.int32, s.shape, 1)
                s = jnp.where(kpos <= qpos, s, NEG_INF)
            m_prev = m_scr[rows, :]
            m_new = jnp.maximum(m_prev, jnp.max(s, axis=1, keepdims=True))
            p = jnp.exp2(s - jnp.tile(m_new, (1, tk // LANES)))
            alpha = jnp.exp2(m_prev - m_new)
            pv = jnp.dot(p.astype(BF16), v_ext, preferred_element_type=F32)
            acc_scr[rows, :] = jnp.tile(alpha, (1, 2)) * acc_scr[rows, :] + pv
            m_scr[rows, :] = m_new


def _attn_prompt_step(lam_init, i, j, qq_ref, k_ref, v_ref, lam_ref, g_ref, o_ref, m_scr, acc_scr):
    tq = qq_ref.shape[1]
    ratio = tq // k_ref.shape[0]

    @pl.when(j == 0)
    def _():
        m_scr[...] = jnp.full(m_scr.shape, NEG_INF, F32)
        acc_scr[...] = jnp.zeros(acc_scr.shape, F32)

    @pl.when(j < ratio * i)
    def _():
        _attn_tile(None, qq_ref, k_ref, v_ref, m_scr, acc_scr)

    for d in range(ratio):
        @pl.when(j == ratio * i + d)
        def _(d=d):
            _attn_tile(d, qq_ref, k_ref, v_ref, m_scr, acc_scr)

    @pl.when(j == ratio * i + ratio - 1)
    def _():
        acc = acc_scr[...]
        o12 = acc[:, :HEAD_V] / acc[:, HEAD_V:]
        o = o12[:tq] - _lam(lam_ref, lam_init) * o12[tq:]
        o_ref[...] = _subln(o, g_ref[...], lam_init).astype(BF16)


ROWS = 2 * ATTN_HEADS
PAGE_ROWS = PAGE_SIZE * ATTN_HEADS


def _attn_sample_step(lam_init, first, last, tail, qd_ref, kn_ref, vn_ref, lam_ref, g_ref, k_refs, v_refs, o_ref,
                      m_scr, l_scr, acc_scr):
    n = len(k_refs)
    qd = qd_ref[...]
    slot_bias = jnp.where(last, NEG_INF, 0.0)

    @pl.when(first)
    def _():
        m_scr[...] = jnp.sum(qd.astype(F32) * kn_ref[...], axis=-1, keepdims=True)
        l_scr[...] = jnp.ones(l_scr.shape, F32)
        acc_scr[...] = vn_ref[...]

    pieces = [lax.dot_general(qd, k_refs[p][...].astype(BF16), (((1,), (1,)), ((), ())),
                              preferred_element_type=F32) for p in range(n)]
    pieces = [sp if p < tail else sp + slot_bias for p, sp in enumerate(pieces)]
    s = jnp.concatenate(pieces, axis=1)
    col = lax.broadcasted_iota(jnp.int32, s.shape, 1)
    row = lax.broadcasted_iota(jnp.int32, s.shape, 0)
    s = jnp.where((col % ATTN_HEADS) == (row % ATTN_HEADS), s, NEG_INF)
    m_prev = m_scr[...]
    m_new = jnp.maximum(m_prev, jnp.max(s, axis=-1, keepdims=True))
    alpha = jnp.exp2(m_prev - m_new)
    p_all = jnp.exp2(s - m_new)
    l_scr[...] = alpha * l_scr[...] + jnp.sum(p_all, axis=-1, keepdims=True)
    pb = p_all.astype(BF16)
    pv = jnp.dot(pb[:, 0:PAGE_ROWS], v_refs[0][...].astype(BF16), preferred_element_type=F32)
    for p in range(1, n):
        pv += jnp.dot(pb[:, p * PAGE_ROWS:(p + 1) * PAGE_ROWS], v_refs[p][...].astype(BF16),
                      preferred_element_type=F32)
    acc_scr[...] = alpha * acc_scr[...] + pv
    m_scr[...] = m_new

    @pl.when(last)
    def _():
        o12 = acc_scr[...] / l_scr[...]
        o = o12[:ATTN_HEADS] - _lam(lam_ref, lam_init) * o12[ATTN_HEADS:]
        o_ref[...] = _subln(o, g_ref[...], lam_init).astype(BF16)


def _attn_kernel(lam_init, n, steps_per_seq, tail, *refs):
    it_ref, jt_ref, _, sq_ref = refs[:4]
    qq_ref, k_ref, v_ref, lam_ref, g_ref, qd_ref, kn_ref, vn_ref = refs[4:12]
    k_refs = refs[12:12 + n]
    v_refs = refs[12 + n:12 + 2 * n]
    o_ref, os_ref = refs[12 + 2 * n:14 + 2 * n]
    m_scr, acc_scr, dm_scr, dl_scr, dacc_scr = refs[14 + 2 * n:]

    step = pl.program_id(2)
    _attn_prompt_step(lam_init, it_ref[step], jt_ref[step], qq_ref, k_ref, v_ref, lam_ref, g_ref, o_ref,
                      m_scr, acc_scr)

    lin = (pl.program_id(0) * pl.num_programs(1) + pl.program_id(1)) * pl.num_programs(2) + step
    t = lin - sq_ref[lin] * steps_per_seq
    _attn_sample_step(lam_init, t == 0, t == steps_per_seq - 1, tail, qd_ref, kn_ref, vn_ref, lam_ref, g_ref,
                      k_refs, v_refs, os_ref, dm_scr, dl_scr, dacc_scr)


def _attn(qq, kb, vb, page_table, qd, k_new, v_new, ck, cv, lam_vecs, g_subln, lam_init, batch, seq, tq, tk):
    m = kb.shape[0]
    nq = seq // tq
    nk = seq // tk
    ratio = tq // tk
    pairs = [(i, j) for i in range(nq) for j in range(ratio * (i + 1))]
    i_tab = jnp.array([p[0] for p in pairs], jnp.int32)
    j_tab = jnp.array([p[1] for p in pairs], jnp.int32)
    npairs = len(pairs)

    bd, n_pages = page_table.shape
    steps = batch * ATTN_HEADS * npairs
    assert steps % bd == 0
    steps_per_seq = steps // bd
    n = -(-n_pages // steps_per_seq)
    tail = n_pages - (steps_per_seq - 1) * n
    assert 0 < tail <= n
    seq_of_step = [t // steps_per_seq for t in range(steps)]
    slot_page = [[min((t % steps_per_seq) * n + p, n_pages - 1) for p in range(n)] for t in range(steps)]
    pages = page_table[jnp.array(seq_of_step, jnp.int32)[:, None], jnp.array(slot_page, jnp.int32)].reshape(steps * n)
    seq_tab = jnp.array(seq_of_step, jnp.int32)

    def lin(b, h, s):
        return (b * ATTN_HEADS + h) * npairs + s

    def page_spec(p):
        return pl.BlockSpec((None, PAGE_ROWS, HEAD_V),
                            lambda b, h, s, it, jt, pg, sq: (pg[lin(b, h, s) * n + p], 0, 0))

    seq_row = lambda b, h, s, it, jt, pg, sq: (sq[lin(b, h, s)], 0, 0)
    fixed = lambda b, h, s, it, jt, pg, sq: (0, 0)
    grid_spec = pltpu.PrefetchScalarGridSpec(
        num_scalar_prefetch=4,
        grid=(batch, ATTN_HEADS, npairs),
        in_specs=[pl.BlockSpec((2, tq, HEAD_V), lambda b, h, s, it, jt, pg, sq: (0, b * nq + it[s], h)),
                  pl.BlockSpec((tk, HEAD_V), lambda b, h, s, it, jt, pg, sq: (b * nk + jt[s], h)),
                  pl.BlockSpec((tk, HEAD_V), lambda b, h, s, it, jt, pg, sq: (b * nk + jt[s], h)),
                  pl.BlockSpec((8, QK_DIM), fixed),
                  pl.BlockSpec((1, HEAD_V), fixed),
                  pl.BlockSpec((None, ROWS, HEAD_V), seq_row),
                  pl.BlockSpec((None, ROWS, HEAD_V), seq_row),
                  pl.BlockSpec((None, ROWS, HEAD_V), seq_row)]
                 + [page_spec(p) for p in range(n)] + [page_spec(p) for p in range(n)],
        out_specs=[pl.BlockSpec((tq, HEAD_V), lambda b, h, s, it, jt, pg, sq: (b * nq + it[s], h)),
                   pl.BlockSpec((None, ATTN_HEADS, HEAD_V), seq_row)],
        scratch_shapes=[pltpu.VMEM((2 * tq, LANES), F32), pltpu.VMEM((2 * tq, 2 * HEAD_V), F32),
                        pltpu.VMEM((ROWS, 1), F32), pltpu.VMEM((ROWS, 1), F32), pltpu.VMEM((ROWS, HEAD_V), F32)],
    )
    return pl.pallas_call(
        functools.partial(_attn_kernel, lam_init, n, steps_per_seq, tail),
        grid_spec=grid_spec,
        out_shape=[jax.ShapeDtypeStruct((m, ATTN_WIDTH), BF16),
                   jax.ShapeDtypeStruct((bd, ATTN_HEADS, HEAD_V), BF16)],
        compiler_params=_cparams("arbitrary", "arbitrary", "arbitrary"),
        name="attn",
    )(i_tab, j_tab, pages, seq_tab, qq, kb, vb, lam_vecs, g_subln, qd, k_new, v_new, *([ck] * n), *([cv] * n))


def _merge_kernel(o_ref, c_ref, ga_ref, gc_ref, x_ref, wa_ref, wc_ref, wo_ref, g_ref, x1_ref, h2_ref):
    a_out = jnp.dot(o_ref[...], wa_ref[...], preferred_element_type=F32)
    c_out = jnp.dot(c_ref[...], wc_ref[...], preferred_element_type=F32)
    merged = ga_ref[...].astype(F32) * a_out + gc_ref[...].astype(F32) * c_out
    x1 = x_ref[...] + jnp.dot(merged.astype(BF16), wo_ref[...], preferred_element_type=F32)
    x1_ref[...] = x1
    h2_ref[...] = _rms(x1, g_ref[...]).astype(BF16)


def _merge(o, c, ga, gc, x, wa, wc, wo, g_ffn, tm):
    m = x.shape[0]
    row = lambda i: (i, 0)
    return pl.pallas_call(
        _merge_kernel,
        grid=(m // tm,),
        in_specs=[pl.BlockSpec((tm, ATTN_WIDTH), row), pl.BlockSpec((tm, CONV_DIM), row),
                  pl.BlockSpec((tm, D_MODEL), row), pl.BlockSpec((tm, D_MODEL), row),
                  pl.BlockSpec((tm, D_MODEL), row),
                  _const_spec((ATTN_WIDTH, D_MODEL)), _const_spec((CONV_DIM, D_MODEL)),
                  _const_spec((D_MODEL, D_MODEL)), _const_spec((1, D_MODEL))],
        out_specs=[pl.BlockSpec((tm, D_MODEL), row), pl.BlockSpec((tm, D_MODEL), row)],
        out_shape=[jax.ShapeDtypeStruct((m, D_MODEL), F32), jax.ShapeDtypeStruct((m, D_MODEL), BF16)],
        compiler_params=_cparams("parallel"),
        name="merge_out",
    )(o, c, ga, gc, x, wa, wc, wo, g_ffn)


FF_BLOCK = 512


def _ffn_kernel(h_ref, x1_ref, wg_ref, wu_ref, wd_ref, gf_ref, y_ref, acc_scr):
    j = pl.program_id(1)

    @pl.when(j == 0)
    def _():
        acc_scr[...] = jnp.zeros(acc_scr.shape, F32)

    h = h_ref[...]
    g = jnp.dot(h, wg_ref[...], preferred_element_type=F32)
    u = jnp.dot(h, wu_ref[...], preferred_element_type=F32)
    act = (g * jax.nn.sigmoid(g) * u).astype(BF16)
    acc_scr[...] += jnp.dot(act, wd_ref[...], preferred_element_type=F32)

    @pl.when(j == pl.num_programs(1) - 1)
    def _():
        y_ref[...] = _rms(x1_ref[...] + acc_scr[...], gf_ref[...])


def _ffn(h2, x1, w_gu, w_down, g_final, tm):
    m = h2.shape[0]
    nj = D_FF // FF_BLOCK
    row = lambda i, j: (i, 0)
    return pl.pallas_call(
        _ffn_kernel,
        grid=(m // tm, nj),
        in_specs=[pl.BlockSpec((tm, D_MODEL), row), pl.BlockSpec((tm, D_MODEL), row),
                  pl.BlockSpec((D_MODEL, FF_BLOCK), lambda i, j: (0, j)),
                  pl.BlockSpec((D_MODEL, FF_BLOCK), lambda i, j: (0, nj + j)),
                  pl.BlockSpec((FF_BLOCK, D_MODEL), lambda i, j: (j, 0)),
                  pl.BlockSpec((1, D_MODEL), lambda i, j: (0, 0))],
        out_specs=pl.BlockSpec((tm, D_MODEL), row),
        out_shape=jax.ShapeDtypeStruct((m, D_MODEL), F32),
        scratch_shapes=[pltpu.VMEM((tm, D_MODEL), F32)],
        compiler_params=_cparams("parallel", "arbitrary"),
        name="ffn",
    )(h2, x1, w_gu, w_gu, w_down, g_final)


def _rope_tables(pos):
    half = QK_DIM // 2
    inv = ROPE_THETA ** (-jnp.arange(half, dtype=F32) / half)
    ang = pos.astype(F32)[:, None] * inv[None, :]
    cos = jnp.tile(jnp.cos(ang), (1, LANES // half))
    sin = jnp.tile(jnp.sin(ang), (1, LANES // half))
    first_half = (jnp.arange(LANES) & half) == 0
    return cos, jnp.where(first_half[None, :], -sin, sin)


def kernel(x_prompt, x_sample, cache_k, cache_v, state_conv, page_table, w_in, w_conv, lambda_q1, lambda_k1, lambda_q2, lambda_k2, g_subln, w_branch_attn, w_branch_conv, w_out, g_mix, g_ffn, w_gate_up, w_down, g_final):
    bp, sp, _ = x_prompt.shape
    bd, sd, _ = x_sample.shape
    assert sd == 1 and w_in.shape[0] == 1
    n_pool = cache_k.shape[1]
    past = page_table.shape[1] * PAGE_SIZE
    lam_init = 0.8 - 0.6 * math.exp(-0.3 * 0)

    w_qkv = w_cv = w_gt = w_in[0].astype(BF16)
    wa = w_branch_attn[0].astype(BF16)
    wc = w_branch_conv[0].astype(BF16)
    wo = w_out[0].astype(BF16)
    w_gu = w_gate_up[0].astype(BF16)
    w_dn = w_down[0].astype(BF16)
    g_mix2 = g_mix[0].reshape(1, D_MODEL)
    g_ffn2 = g_ffn[0].reshape(1, D_MODEL)
    g_fin2 = g_final.reshape(1, D_MODEL)
    g_sub2 = g_subln[0].reshape(1, HEAD_V)
    lam_vecs = jnp.concatenate([lambda_q1, lambda_k1, lambda_q2, lambda_k2, jnp.zeros((4, QK_DIM), F32)], axis=0)
    wcv3 = w_conv[0]

    mp = bp * sp
    xp = x_prompt.reshape(mp, D_MODEL)
    cos_p, sin_p = _rope_tables(jnp.arange(sp, dtype=jnp.int32))
    tm = 512
    hp, qq, k_p, kb, v_p, vb = _qkv(xp, g_mix2, w_qkv, cos_p, sin_p, tm)
    c_p, st_p = _conv_prompt(hp, w_cv, wcv3, tm, sp)
    ga_p, gc_p = _gates(hp, w_gt, tm)

    xs = x_sample.reshape(bd, D_MODEL)
    cos_s, sin_s = _rope_tables(jnp.full((bd,), past, dtype=jnp.int32))
    hs, qq_s, k_s, _, v_s, _ = _qkv(xs, g_mix2, w_qkv, cos_s, sin_s, bd)
    c_s, st_s = _conv_sample(hs, w_cv, wcv3, state_conv[0].reshape(bd, 2 * CONV_DIM))
    ga_s, gc_s = _gates(hs, w_gt, bd)
    qd = jnp.transpose(qq_s.reshape(2, bd, ATTN_HEADS, HEAD_V), (1, 0, 2, 3)).reshape(bd, ROWS, HEAD_V)
    k_new = jnp.tile(k_s.reshape(bd, ATTN_HEADS, HEAD_V), (1, 2, 1))
    v_new = jnp.tile(v_s.reshape(bd, ATTN_HEADS, HEAD_V), (1, 2, 1))

    o_p, o_s = _attn(qq, kb, vb, page_table, qd, k_new, v_new,
                     cache_k.reshape(n_pool, PAGE_ROWS, HEAD_V), cache_v.reshape(n_pool, PAGE_ROWS, HEAD_V),
                     lam_vecs, g_sub2, lam_init, bp, sp, 1024, 512)
    x1_p, h2_p = _merge(o_p, c_p, ga_p, gc_p, xp, wa, wc, wo, g_ffn2, 256)
    y_p = _ffn(h2_p, x1_p, w_gu, w_dn, g_fin2, tm)
    x1_s, h2_s = _merge(o_s.reshape(bd, ATTN_WIDTH), c_s, ga_s, gc_s, xs, wa, wc, wo, g_ffn2, bd)
    y_s = _ffn(h2_s, x1_s, w_gu, w_dn, g_fin2, bd)

    return (y_p.reshape(bp, sp, D_MODEL),
            y_s.reshape(bd, sd, D_MODEL),
            k_p.reshape(1, bp, sp, ATTN_HEADS, HEAD_V),
            v_p.reshape(1, bp, sp, ATTN_HEADS, HEAD_V),
            st_p[:, HALO - 2:, :].reshape(1, bp, CONV_WIDTH - 1, CONV_DIM),
            k_s.reshape(1, bd, sd, ATTN_HEADS, HEAD_V),
            v_s.reshape(1, bd, sd, ATTN_HEADS, HEAD_V),
            st_s.reshape(1, bd, CONV_WIDTH - 1, CONV_DIM))
```

```python
import functools
import math

import jax
import jax.numpy as jnp
from jax import lax
from jax.experimental import pallas as pl
from jax.experimental.pallas import tpu as pltpu

F32 = jnp.float32
BF16 = jnp.bfloat16

D_MODEL = 2048
ATTN_HEADS = 8
QK_DIM = 64
HEAD_V = 128
ATTN_WIDTH = ATTN_HEADS * HEAD_V
CONV_DIM = D_MODEL // 2
CONV_WIDTH = 3
D_FF = 5632
ROPE_THETA = 10000.0
PAGE_SIZE = 128
EPS = 1e-5
SCALE = QK_DIM ** -0.5
Q_SCALE = SCALE * math.log2(math.e)
NEG_INF = -1e30
LANES = 128

VMEM_LIMIT = 56 * 1024 * 1024


def _cparams(*sem):
    return pltpu.CompilerParams(dimension_semantics=sem, vmem_limit_bytes=VMEM_LIMIT)


def _const_spec(shape):
    nd = len(shape)
    return pl.BlockSpec(shape, lambda *_: (0,) * nd, pipeline_mode=pl.Buffered(1))


def _col_block_spec(rows, width, block):
    return pl.BlockSpec((rows, width), lambda *_: (0, block), pipeline_mode=pl.Buffered(1))


QKV_BLOCK = (3 * ATTN_WIDTH, 0)
CONV_BLOCK = (3 * CONV_DIM, 1)
GATE_A_BLOCK = (D_MODEL, 3)
GATE_C_BLOCK = (D_MODEL, 4)
assert QKV_BLOCK[0] == CONV_BLOCK[0] and GATE_A_BLOCK[1] * D_MODEL == QKV_BLOCK[0] + CONV_BLOCK[0]


def _rms(x, g):
    return x * lax.rsqrt(jnp.mean(x * x, axis=-1, keepdims=True) + EPS) * g


def _rope(z, cos, sin_signed, first_half):
    partner = jnp.where(first_half, pltpu.roll(z, 96, 1), pltpu.roll(z, 32, 1))
    return z * cos + partner * sin_signed


def _qkv_kernel(x_ref, g_ref, w_ref, cos_ref, sin_ref, h_ref, qq_ref, k_ref, kb_ref, v_ref, vb_ref):
    h = _rms(x_ref[...], g_ref[...]).astype(BF16)
    h_ref[...] = h
    z = jnp.dot(h, w_ref[...], preferred_element_type=F32)
    cos = cos_ref[...]
    sin = sin_ref[...]
    lane = lax.broadcasted_iota(jnp.int32, cos.shape, 1)
    first_half = (lane & 32) == 0
    map1 = lane < QK_DIM
    for hh in range(ATTN_HEADS):
        lo = hh * HEAD_V
        q = _rope(z[:, lo:lo + HEAD_V], cos, sin, first_half) * Q_SCALE
        qq_ref[0, :, lo:lo + HEAD_V] = jnp.where(map1, q, 0.0).astype(BF16)
        qq_ref[1, :, lo:lo + HEAD_V] = jnp.where(map1, 0.0, q).astype(BF16)
        k = _rope(z[:, ATTN_WIDTH + lo:ATTN_WIDTH + lo + HEAD_V], cos, sin, first_half)
        k_ref[:, lo:lo + HEAD_V] = k
        kb_ref[:, lo:lo + HEAD_V] = k.astype(BF16)
    v = z[:, 2 * ATTN_WIDTH:]
    v_ref[...] = v
    vb_ref[...] = v.astype(BF16)


def _qkv(x, g, w, cos, sin_signed, tm):
    m = x.shape[0]
    nt = cos.shape[0] // tm
    row = lambda i: (i, 0)
    return pl.pallas_call(
        _qkv_kernel,
        grid=(m // tm,),
        in_specs=[pl.BlockSpec((tm, D_MODEL), row),
                  _const_spec((1, D_MODEL)),
                  _col_block_spec(D_MODEL, *QKV_BLOCK),
                  pl.BlockSpec((tm, LANES), lambda i: (i % nt, 0)),
                  pl.BlockSpec((tm, LANES), lambda i: (i % nt, 0))],
        out_specs=[pl.BlockSpec((tm, D_MODEL), row),
                   pl.BlockSpec((2, tm, ATTN_WIDTH), lambda i: (0, i, 0)),
                   pl.BlockSpec((tm, ATTN_WIDTH), row), pl.BlockSpec((tm, ATTN_WIDTH), row),
                   pl.BlockSpec((tm, ATTN_WIDTH), row), pl.BlockSpec((tm, ATTN_WIDTH), row)],
        out_shape=[jax.ShapeDtypeStruct((m, D_MODEL), BF16),
                   jax.ShapeDtypeStruct((2, m, ATTN_WIDTH), BF16),
                   jax.ShapeDtypeStruct((m, ATTN_WIDTH), F32), jax.ShapeDtypeStruct((m, ATTN_WIDTH), BF16),
                   jax.ShapeDtypeStruct((m, ATTN_WIDTH), F32), jax.ShapeDtypeStruct((m, ATTN_WIDTH), BF16)],
        compiler_params=_cparams("parallel"),
        name="qkv_rope",
    )(x, g, w, cos, sin_signed)


HALO = 8


def _conv_prompt_kernel(tiles_per_seq, h_ref, w_ref, wc_ref, c_ref, st_ref, u_scr):
    i = pl.program_id(0)
    tm = h_ref.shape[0]
    z = jnp.dot(h_ref[...], w_ref[...], preferred_element_type=F32)
    u = z[:, CONV_DIM:2 * CONV_DIM] * z[:, 2 * CONV_DIM:]

    @pl.when(i % tiles_per_seq == 0)
    def _():
        u_scr[0:HALO, :] = jnp.zeros((HALO, CONV_DIM), F32)

    @pl.when(i % tiles_per_seq != 0)
    def _():
        u_scr[0:HALO, :] = u_scr[tm:tm + HALO, :]

    u_scr[HALO:HALO + tm, :] = u
    wc = wc_ref[...]
    y = (wc[0:1, :] * u_scr[HALO - 2:HALO - 2 + tm, :]
         + wc[1:2, :] * u_scr[HALO - 1:HALO - 1 + tm, :]
         + wc[2:3, :] * u)
    c_ref[...] = (z[:, :CONV_DIM] * y).astype(BF16)
    st_ref[...] = u[tm - HALO:, :]


def _conv_prompt(h, w, w_conv, tm, seq):
    m = h.shape[0]
    tiles_per_seq = seq // tm
    return pl.pallas_call(
        functools.partial(_conv_prompt_kernel, tiles_per_seq),
        grid=(m // tm,),
        in_specs=[pl.BlockSpec((tm, D_MODEL), lambda i: (i, 0)),
                  _col_block_spec(D_MODEL, *CONV_BLOCK),
                  _const_spec((CONV_WIDTH, CONV_DIM))],
        out_specs=[pl.BlockSpec((tm, CONV_DIM), lambda i: (i, 0)),
                   pl.BlockSpec((None, HALO, CONV_DIM), lambda i: (i // tiles_per_seq, 0, 0))],
        out_shape=[jax.ShapeDtypeStruct((m, CONV_DIM), BF16),
                   jax.ShapeDtypeStruct((m // seq, HALO, CONV_DIM), F32)],
        scratch_shapes=[pltpu.VMEM((tm + HALO, CONV_DIM), F32)],
        compiler_params=_cparams("arbitrary"),
        name="conv_prompt",
    )(h, w, w_conv)


def _conv_sample_kernel(h_ref, w_ref, wc_ref, st_ref, c_ref, nst_ref):
    z = jnp.dot(h_ref[...], w_ref[...], preferred_element_type=F32)
    u = z[:, CONV_DIM:2 * CONV_DIM] * z[:, 2 * CONV_DIM:]
    wc = wc_ref[...]
    s0 = st_ref[:, :CONV_DIM]
    s1 = st_ref[:, CONV_DIM:]
    y = wc[0:1, :] * s0 + wc[1:2, :] * s1 + wc[2:3, :] * u
    c_ref[...] = (z[:, :CONV_DIM] * y).astype(BF16)
    nst_ref[:, :CONV_DIM] = s1
    nst_ref[:, CONV_DIM:] = u


def _conv_sample(h, w, w_conv, state2d):
    m = h.shape[0]
    return pl.pallas_call(
        _conv_sample_kernel,
        grid=(1,),
        in_specs=[_const_spec((m, D_MODEL)), _col_block_spec(D_MODEL, *CONV_BLOCK),
                  _const_spec((CONV_WIDTH, CONV_DIM)), _const_spec((m, 2 * CONV_DIM))],
        out_specs=[pl.BlockSpec((m, CONV_DIM), lambda i: (0, 0)),
                   pl.BlockSpec((m, 2 * CONV_DIM), lambda i: (0, 0))],
        out_shape=[jax.ShapeDtypeStruct((m, CONV_DIM), BF16),
                   jax.ShapeDtypeStruct((m, 2 * CONV_DIM), F32)],
        compiler_params=_cparams("arbitrary"),
        name="conv_sample",
    )(h, w, w_conv, state2d)


def _gate_kernel(h_ref, wa_ref, wc_ref, ga_ref, gc_ref):
    h = h_ref[...]
    ga_ref[...] = jax.nn.sigmoid(jnp.dot(h, wa_ref[...], preferred_element_type=F32)).astype(BF16)
    gc_ref[...] = jax.nn.sigmoid(jnp.dot(h, wc_ref[...], preferred_element_type=F32)).astype(BF16)


def _gates(h, w, tm):
    m = h.shape[0]
    row = lambda i: (i, 0)
    return pl.pallas_call(
        _gate_kernel,
        grid=(m // tm,),
        in_specs=[pl.BlockSpec((tm, D_MODEL), row),
                  _col_block_spec(D_MODEL, *GATE_A_BLOCK), _col_block_spec(D_MODEL, *GATE_C_BLOCK)],
        out_specs=[pl.BlockSpec((tm, D_MODEL), row), pl.BlockSpec((tm, D_MODEL), row)],
        out_shape=[jax.ShapeDtypeStruct((m, D_MODEL), BF16), jax.ShapeDtypeStruct((m, D_MODEL), BF16)],
        compiler_params=_cparams("parallel"),
        name="gates",
    )(h, w, w)


def _lam(lam_ref, lam_init):
    lv = lam_ref[...]
    a = jnp.sum(lv[0:1, :] * lv[1:2, :], axis=-1, keepdims=True)
    b = jnp.sum(lv[2:3, :] * lv[3:4, :], axis=-1, keepdims=True)
    return jnp.exp(a) - jnp.exp(b) + lam_init


def _subln(o, g, lam_init):
    return o * lax.rsqrt(jnp.mean(o * o, axis=-1, keepdims=True) + EPS) * g * (1.0 - lam_init)


ROW_BLOCK = 128


def _attn_tile(diag, qq_ref, k_ref, v_ref, m_scr, acc_scr):
    tq = qq_ref.shape[1]
    tk = k_ref.shape[0]
    k = k_ref[...]
    v_ext = jnp.concatenate([v_ref[...], jnp.ones((tk, HEAD_V), BF16)], axis=1)
    for mp in range(2):
        for r0 in range(0, tq, ROW_BLOCK):
            c0 = 0 if diag is None else diag * tk
            if r0 + ROW_BLOCK - 1 < c0:
                continue
            rows = pl.ds(mp * tq + r0, ROW_BLOCK)
            s = lax.dot_general(qq_ref[mp, r0:r0 + ROW_BLOCK, :], k, (((1,), (1,)), ((), ())),
                                preferred_element_type=F32)
            if diag is not None and r0 < c0 + tk - 1:
                qpos = r0 + lax.broadcasted_iota(jnp.int32, s.shape, 0)
                kpos = c0 + lax.broadcasted_iota(jnp.int32, s.shape, 1)
                s = jnp.where(kpos <= qpos, s, NEG_INF)
            m_prev = m_scr[rows, :]
            m_new = jnp.maximum(m_prev, jnp.max(s, axis=1, keepdims=True))
            p = jnp.exp2(s - jnp.tile(m_new, (1, tk // LANES)))
            alpha = jnp.exp2(m_prev - m_new)
            pv = jnp.dot(p.astype(BF16), v_ext, preferred_element_type=F32)
            acc_scr[rows, :] = jnp.tile(alpha, (1, 2)) * acc_scr[rows, :] + pv
            m_scr[rows, :] = m_new


def _attn_prompt_step(lam_init, i, j, qq_ref, k_ref, v_ref, lam_ref, g_ref, o_ref, m_scr, acc_scr):
    tq = qq_ref.shape[1]
    ratio = tq // k_ref.shape[0]

    @pl.when(j == 0)
    def _():
        m_scr[...] = jnp.full(m_scr.shape, NEG_INF, F32)
        acc_scr[...] = jnp.zeros(acc_scr.shape, F32)

    @pl.when(j < ratio * i)
    def _():
        _attn_tile(None, qq_ref, k_ref, v_ref, m_scr, acc_scr)

    for d in range(ratio):
        @pl.when(j == ratio * i + d)
        def _(d=d):
            _attn_tile(d, qq_ref, k_ref, v_ref, m_scr, acc_scr)

    @pl.when(j == ratio * i + ratio - 1)
    def _():
        acc = acc_scr[...]
        o12 = acc[:, :HEAD_V] / acc[:, HEAD_V:]
        o = o12[:tq] - _lam(lam_ref, lam_init) * o12[tq:]
        o_ref[...] = _subln(o, g_ref[...], lam_init).astype(BF16)


ROWS = 2 * ATTN_HEADS
PAGE_ROWS = PAGE_SIZE * ATTN_HEADS


def _attn_sample_step(lam_init, first, last, tail, qd_ref, kn_ref, vn_ref, lam_ref, g_ref, k_refs, v_refs, o_ref,
                      m_scr, l_scr, acc_scr):
    n = len(k_refs)
    qd = qd_ref[...]
    slot_bias = jnp.where(last, NEG_INF, 0.0)

    @pl.when(first)
    def _():
        m_scr[...] = jnp.sum(qd.astype(F32) * kn_ref[...], axis=-1, keepdims=True)
        l_scr[...] = jnp.ones(l_scr.shape, F32)
        acc_scr[...] = vn_ref[...]

    pieces = [lax.dot_general(qd, k_refs[p][...].astype(BF16), (((1,), (1,)), ((), ())),
                              preferred_element_type=F32) for p in range(n)]
    pieces = [sp if p < tail else sp + slot_bias for p, sp in enumerate(pieces)]
    s = jnp.concatenate(pieces, axis=1)
    col = lax.broadcasted_iota(jnp.int32, s.shape, 1)
    row = lax.broadcasted_iota(jnp.int32, s.shape, 0)
    s = jnp.where((col % ATTN_HEADS) == (row % ATTN_HEADS), s, NEG_INF)
    m_prev = m_scr[...]
    m_new = jnp.maximum(m_prev, jnp.max(s, axis=-1, keepdims=True))
    alpha = jnp.exp2(m_prev - m_new)
    p_all = jnp.exp2(s - m_new)
    l_scr[...] = alpha * l_scr[...] + jnp.sum(p_all, axis=-1, keepdims=True)
    pb = p_all.astype(BF16)
    pv = jnp.dot(pb[:, 0:PAGE_ROWS], v_refs[0][...].astype(BF16), preferred_element_type=F32)
    for p in range(1, n):
        pv += jnp.dot(pb[:, p * PAGE_ROWS:(p + 1) * PAGE_ROWS], v_refs[p][...].astype(BF16),
                      preferred_element_type=F32)
    acc_scr[...] = alpha * acc_scr[...] + pv
    m_scr[...] = m_new

    @pl.when(last)
    def _():
        o12 = acc_scr[...] / l_scr[...]
        o = o12[:ATTN_HEADS] - _lam(lam_ref, lam_init) * o12[ATTN_HEADS:]
        o_ref[...] = _subln(o, g_ref[...], lam_init).astype(BF16)


def _page_copies(pg_ref, ck_hbm, cv_hbm, kbuf, vbuf, sem, chunk, slot, n):
    copies = []
    for p in range(n):
        page = pg_ref[chunk * n + p]
        copies.append(pltpu.make_async_copy(ck_hbm.at[page], kbuf.at[slot, p], sem.at[0, slot]))
        copies.append(pltpu.make_async_copy(cv_hbm.at[page], vbuf.at[slot, p], sem.at[1, slot]))
    return copies


def _attn_kernel(lam_init, n, steps_per_seq, tail, it_ref, jt_ref, pg_ref, sq_ref,
                 qq_ref, k_ref, v_ref, lam_ref, g_ref, qd_ref, kn_ref, vn_ref, ck_hbm, cv_hbm,
                 o_ref, os_ref, m_scr, acc_scr, dm_scr, dl_scr, dacc_scr, kbuf, vbuf, sem):
    step = pl.program_id(2)
    steps = pl.num_programs(0) * pl.num_programs(1) * pl.num_programs(2)
    lin = (pl.program_id(0) * pl.num_programs(1) + pl.program_id(1)) * pl.num_programs(2) + step
    slot = lin % 2
    copies = functools.partial(_page_copies, pg_ref, ck_hbm, cv_hbm, kbuf, vbuf, sem)

    @pl.when(lin == 0)
    def _():
        for c in copies(0, 0, n):
            c.start()

    @pl.when(lin + 1 < steps)
    def _():
        for c in copies(lin + 1, 1 - slot, n):
            c.start()

    _attn_prompt_step(lam_init, it_ref[step], jt_ref[step], qq_ref, k_ref, v_ref, lam_ref, g_ref, o_ref,
                      m_scr, acc_scr)

    for c in copies(lin, slot, n):
        c.wait()
    t = lin - sq_ref[lin] * steps_per_seq
    _attn_sample_step(lam_init, t == 0, t == steps_per_seq - 1, tail, qd_ref, kn_ref, vn_ref, lam_ref, g_ref,
                      [kbuf.at[slot, p] for p in range(n)], [vbuf.at[slot, p] for p in range(n)],
                      os_ref, dm_scr, dl_scr, dacc_scr)


def _attn(qq, kb, vb, page_table, qd, k_new, v_new, ck, cv, lam_vecs, g_subln, lam_init, batch, seq, tq, tk):
    m = kb.shape[0]
    nq = seq // tq
    nk = seq // tk
    ratio = tq // tk
    pairs = [(i, j) for i in range(nq) for j in range(ratio * (i + 1))]
    i_tab = jnp.array([p[0] for p in pairs], jnp.int32)
    j_tab = jnp.array([p[1] for p in pairs], jnp.int32)
    npairs = len(pairs)

    bd, n_pages = page_table.shape
    steps = batch * ATTN_HEADS * npairs
    assert steps % bd == 0
    steps_per_seq = steps // bd
    n = -(-n_pages // steps_per_seq)
    tail = n_pages - (steps_per_seq - 1) * n
    assert 0 < tail <= n
    seq_of_step = [t // steps_per_seq for t in range(steps)]
    slot_page = [[min((t % steps_per_seq) * n + p, n_pages - 1) for p in range(n)] for t in range(steps)]
    pages = page_table[jnp.array(seq_of_step, jnp.int32)[:, None], jnp.array(slot_page, jnp.int32)].reshape(steps * n)
    seq_tab = jnp.array(seq_of_step, jnp.int32)

    def lin(b, h, s):
        return (b * ATTN_HEADS + h) * npairs + s

    seq_row = lambda b, h, s, it, jt, pg, sq: (sq[lin(b, h, s)], 0, 0)
    fixed = lambda b, h, s, it, jt, pg, sq: (0, 0)
    grid_spec = pltpu.PrefetchScalarGridSpec(
        num_scalar_prefetch=4,
        grid=(batch, ATTN_HEADS, npairs),
        in_specs=[pl.BlockSpec((2, tq, HEAD_V), lambda b, h, s, it, jt, pg, sq: (0, b * nq + it[s], h)),
                  pl.BlockSpec((tk, HEAD_V), lambda b, h, s, it, jt, pg, sq: (b * nk + jt[s], h)),
                  pl.BlockSpec((tk, HEAD_V), lambda b, h, s, it, jt, pg, sq: (b * nk + jt[s], h)),
                  pl.BlockSpec((8, QK_DIM), fixed),
                  pl.BlockSpec((1, HEAD_V), fixed),
                  pl.BlockSpec((None, ROWS, HEAD_V), seq_row),
                  pl.BlockSpec((None, ROWS, HEAD_V), seq_row),
                  pl.BlockSpec((None, ROWS, HEAD_V), seq_row),
                  pl.BlockSpec(memory_space=pl.ANY), pl.BlockSpec(memory_space=pl.ANY)],
        out_specs=[pl.BlockSpec((tq, HEAD_V), lambda b, h, s, it, jt, pg, sq: (b * nq + it[s], h)),
                   pl.BlockSpec((None, ATTN_HEADS, HEAD_V), seq_row)],
        scratch_shapes=[pltpu.VMEM((2 * tq, LANES), F32), pltpu.VMEM((2 * tq, 2 * HEAD_V), F32),
                        pltpu.VMEM((ROWS, 1), F32), pltpu.VMEM((ROWS, 1), F32), pltpu.VMEM((ROWS, HEAD_V), F32),
                        pltpu.VMEM((2, n, PAGE_ROWS, HEAD_V), F32), pltpu.VMEM((2, n, PAGE_ROWS, HEAD_V), F32),
                        pltpu.SemaphoreType.DMA((2, 2))],
    )
    return pl.pallas_call(
        functools.partial(_attn_kernel, lam_init, n, steps_per_seq, tail),
        grid_spec=grid_spec,
        out_shape=[jax.ShapeDtypeStruct((m, ATTN_WIDTH), BF16),
                   jax.ShapeDtypeStruct((bd, ATTN_HEADS, HEAD_V), BF16)],
        compiler_params=_cparams("arbitrary", "arbitrary", "arbitrary"),
        name="attn",
    )(i_tab, j_tab, pages, seq_tab, qq, kb, vb, lam_vecs, g_subln, qd, k_new, v_new, ck, cv)


def _merge_kernel(o_ref, c_ref, ga_ref, gc_ref, x_ref, wa_ref, wc_ref, wo_ref, g_ref, x1_ref, h2_ref):
    a_out = jnp.dot(o_ref[...], wa_ref[...], preferred_element_type=F32)
    c_out = jnp.dot(c_ref[...], wc_ref[...], preferred_element_type=F32)
    merged = ga_ref[...].astype(F32) * a_out + gc_ref[...].astype(F32) * c_out
    x1 = x_ref[...] + jnp.dot(merged.astype(BF16), wo_ref[...], preferred_element_type=F32)
    x1_ref[...] = x1
    h2_ref[...] = _rms(x1, g_ref[...]).astype(BF16)


def _merge(o, c, ga, gc, x, wa, wc, wo, g_ffn, tm):
    m = x.shape[0]
    row = lambda i: (i, 0)
    return pl.pallas_call(
        _merge_kernel,
        grid=(m // tm,),
        in_specs=[pl.BlockSpec((tm, ATTN_WIDTH), row), pl.BlockSpec((tm, CONV_DIM), row),
                  pl.BlockSpec((tm, D_MODEL), row), pl.BlockSpec((tm, D_MODEL), row),
                  pl.BlockSpec((tm, D_MODEL), row),
                  _const_spec((ATTN_WIDTH, D_MODEL)), _const_spec((CONV_DIM, D_MODEL)),
                  _const_spec((D_MODEL, D_MODEL)), _const_spec((1, D_MODEL))],
        out_specs=[pl.BlockSpec((tm, D_MODEL), row), pl.BlockSpec((tm, D_MODEL), row)],
        out_shape=[jax.ShapeDtypeStruct((m, D_MODEL), F32), jax.ShapeDtypeStruct((m, D_MODEL), BF16)],
        compiler_params=_cparams("parallel"),
        name="merge_out",
    )(o, c, ga, gc, x, wa, wc, wo, g_ffn)


FF_BLOCK = 512


def _ffn_kernel(h_ref, x1_ref, wg_ref, wu_ref, wd_ref, gf_ref, y_ref, acc_scr):
    j = pl.program_id(1)

    @pl.when(j == 0)
    def _():
        acc_scr[...] = jnp.zeros(acc_scr.shape, F32)

    h = h_ref[...]
    g = jnp.dot(h, wg_ref[...], preferred_element_type=F32)
    u = jnp.dot(h, wu_ref[...], preferred_element_type=F32)
    act = (g * jax.nn.sigmoid(g) * u).astype(BF16)
    acc_scr[...] += jnp.dot(act, wd_ref[...], preferred_element_type=F32)

    @pl.when(j == pl.num_programs(1) - 1)
    def _():
        y_ref[...] = _rms(x1_ref[...] + acc_scr[...], gf_ref[...])


def _ffn(h2, x1, w_gu, w_down, g_final, tm):
    m = h2.shape[0]
    nj = D_FF // FF_BLOCK
    row = lambda i, j: (i, 0)
    return pl.pallas_call(
        _ffn_kernel,
        grid=(m // tm, nj),
        in_specs=[pl.BlockSpec((tm, D_MODEL), row), pl.BlockSpec((tm, D_MODEL), row),
                  pl.BlockSpec((D_MODEL, FF_BLOCK), lambda i, j: (0, j)),
                  pl.BlockSpec((D_MODEL, FF_BLOCK), lambda i, j: (0, nj + j)),
                  pl.BlockSpec((FF_BLOCK, D_MODEL), lambda i, j: (j, 0)),
                  pl.BlockSpec((1, D_MODEL), lambda i, j: (0, 0))],
        out_specs=pl.BlockSpec((tm, D_MODEL), row),
        out_shape=jax.ShapeDtypeStruct((m, D_MODEL), F32),
        scratch_shapes=[pltpu.VMEM((tm, D_MODEL), F32)],
        compiler_params=_cparams("parallel", "arbitrary"),
        name="ffn",
    )(h2, x1, w_gu, w_gu, w_down, g_final)


def _rope_tables(pos):
    half = QK_DIM // 2
    inv = ROPE_THETA ** (-jnp.arange(half, dtype=F32) / half)
    ang = pos.astype(F32)[:, None] * inv[None, :]
    cos = jnp.tile(jnp.cos(ang), (1, LANES // half))
    sin = jnp.tile(jnp.sin(ang), (1, LANES // half))
    first_half = (jnp.arange(LANES) & half) == 0
    return cos, jnp.where(first_half[None, :], -sin, sin)


def kernel(x_prompt, x_sample, cache_k, cache_v, state_conv, page_table, w_in, w_conv, lambda_q1, lambda_k1, lambda_q2, lambda_k2, g_subln, w_branch_attn, w_branch_conv, w_out, g_mix, g_ffn, w_gate_up, w_down, g_final):
    bp, sp, _ = x_prompt.shape
    bd, sd, _ = x_sample.shape
    assert sd == 1 and w_in.shape[0] == 1
    n_pool = cache_k.shape[1]
    past = page_table.shape[1] * PAGE_SIZE
    lam_init = 0.8 - 0.6 * math.exp(-0.3 * 0)

    w_qkv = w_cv = w_gt = w_in[0].astype(BF16)
    wa = w_branch_attn[0].astype(BF16)
    wc = w_branch_conv[0].astype(BF16)
    wo = w_out[0].astype(BF16)
    w_gu = w_gate_up[0].astype(BF16)
    w_dn = w_down[0].astype(BF16)
    g_mix2 = g_mix[0].reshape(1, D_MODEL)
    g_ffn2 = g_ffn[0].reshape(1, D_MODEL)
    g_fin2 = g_final.reshape(1, D_MODEL)
    g_sub2 = g_subln[0].reshape(1, HEAD_V)
    lam_vecs = jnp.concatenate([lambda_q1, lambda_k1, lambda_q2, lambda_k2, jnp.zeros((4, QK_DIM), F32)], axis=0)
    wcv3 = w_conv[0]

    mp = bp * sp
    xp = x_prompt.reshape(mp, D_MODEL)
    cos_p, sin_p = _rope_tables(jnp.arange(sp, dtype=jnp.int32))
    tm = 512
    hp, qq, k_p, kb, v_p, vb = _qkv(xp, g_mix2, w_qkv, cos_p, sin_p, tm)
    c_p, st_p = _conv_prompt(hp, w_cv, wcv3, tm, sp)
    ga_p, gc_p = _gates(hp, w_gt, tm)

    xs = x_sample.reshape(bd, D_MODEL)
    cos_s, sin_s = _rope_tables(jnp.full((bd,), past, dtype=jnp.int32))
    hs, qq_s, k_s, _, v_s, _ = _qkv(xs, g_mix2, w_qkv, cos_s, sin_s, bd)
    c_s, st_s = _conv_sample(hs, w_cv, wcv3, state_conv[0].reshape(bd, 2 * CONV_DIM))
    ga_s, gc_s = _gates(hs, w_gt, bd)
    qd = jnp.transpose(qq_s.reshape(2, bd, ATTN_HEADS, HEAD_V), (1, 0, 2, 3)).reshape(bd, ROWS, HEAD_V)
    k_new = jnp.tile(k_s.reshape(bd, ATTN_HEADS, HEAD_V), (1, 2, 1))
    v_new = jnp.tile(v_s.reshape(bd, ATTN_HEADS, HEAD_V), (1, 2, 1))

    o_p, o_s = _attn(qq, kb, vb, page_table, qd, k_new, v_new,
                     cache_k.reshape(n_pool, PAGE_ROWS, HEAD_V), cache_v.reshape(n_pool, PAGE_ROWS, HEAD_V),
                     lam_vecs, g_sub2, lam_init, bp, sp, 1024, 512)
    x1_p, h2_p = _merge(o_p, c_p, ga_p, gc_p, xp, wa, wc, wo, g_ffn2, 256)
    y_p = _ffn(h2_p, x1_p, w_gu, w_dn, g_fin2, tm)
    x1_s, h2_s = _merge(o_s.reshape(bd, ATTN_WIDTH), c_s, ga_s, gc_s, xs, wa, wc, wo, g_ffn2, bd)
    y_s = _ffn(h2_s, x1_s, w_gu, w_dn, g_fin2, bd)

    return (y_p.reshape(bp, sp, D_MODEL),
            y_s.reshape(bd, sd, D_MODEL),
            k_p.reshape(1, bp, sp, ATTN_HEADS, HEAD_V),
            v_p.reshape(1, bp, sp, ATTN_HEADS, HEAD_V),
            st_p[:, HALO - 2:, :].reshape(1, bp, CONV_WIDTH - 1, CONV_DIM),
            k_s.reshape(1, bd, sd, ATTN_HEADS, HEAD_V),
            v_s.reshape(1, bd, sd, ATTN_HEADS, HEAD_V),
            st_s.reshape(1, bd, CONV_WIDTH - 1, CONV_DIM))
```

```python
import functools
import math

import jax
import jax.numpy as jnp
from jax import lax
from jax.experimental import pallas as pl
from jax.experimental.pallas import tpu as pltpu

F32 = jnp.float32
BF16 = jnp.bfloat16

D_MODEL = 2048
ATTN_HEADS = 8
QK_DIM = 64
HEAD_V = 128
ATTN_WIDTH = ATTN_HEADS * HEAD_V
CONV_DIM = D_MODEL // 2
CONV_WIDTH = 3
D_FF = 5632
ROPE_THETA = 10000.0
PAGE_SIZE = 128
EPS = 1e-5
SCALE = QK_DIM ** -0.5
Q_SCALE = SCALE * math.log2(math.e)
NEG_INF = -1e30
LANES = 128

VMEM_LIMIT = 56 * 1024 * 1024


def _cparams(*sem):
    return pltpu.CompilerParams(dimension_semantics=sem, vmem_limit_bytes=VMEM_LIMIT)


def _const_spec(shape):
    nd = len(shape)
    return pl.BlockSpec(shape, lambda *_: (0,) * nd, pipeline_mode=pl.Buffered(1))


def _col_block_spec(rows, width, block):
    return pl.BlockSpec((rows, width), lambda *_: (0, block), pipeline_mode=pl.Buffered(1))


QKV_BLOCK = (3 * ATTN_WIDTH, 0)
CONV_BLOCK = (3 * CONV_DIM, 1)
GATE_A_BLOCK = (D_MODEL, 3)
GATE_C_BLOCK = (D_MODEL, 4)
assert QKV_BLOCK[0] == CONV_BLOCK[0] and GATE_A_BLOCK[1] * D_MODEL == QKV_BLOCK[0] + CONV_BLOCK[0]


def _rms(x, g):
    return x * lax.rsqrt(jnp.mean(x * x, axis=-1, keepdims=True) + EPS) * g


def _rope(z, cos, sin_signed, first_half):
    partner = jnp.where(first_half, pltpu.roll(z, 96, 1), pltpu.roll(z, 32, 1))
    return z * cos + partner * sin_signed


def _qkv_kernel(x_ref, g_ref, w_ref, cos_ref, sin_ref, h_ref, qq_ref, k_ref, kb_ref, v_ref, vb_ref):
    h = _rms(x_ref[...], g_ref[...]).astype(BF16)
    h_ref[...] = h
    z = jnp.dot(h, w_ref[...], preferred_element_type=F32)
    cos = cos_ref[...]
    sin = sin_ref[...]
    lane = lax.broadcasted_iota(jnp.int32, cos.shape, 1)
    first_half = (lane & 32) == 0
    map1 = lane < QK_DIM
    for hh in range(ATTN_HEADS):
        lo = hh * HEAD_V
        q = _rope(z[:, lo:lo + HEAD_V], cos, sin, first_half) * Q_SCALE
        qq_ref[0, :, lo:lo + HEAD_V] = jnp.where(map1, q, 0.0).astype(BF16)
        qq_ref[1, :, lo:lo + HEAD_V] = jnp.where(map1, 0.0, q).astype(BF16)
        k = _rope(z[:, ATTN_WIDTH + lo:ATTN_WIDTH + lo + HEAD_V], cos, sin, first_half)
        k_ref[:, lo:lo + HEAD_V] = k
        kb_ref[:, lo:lo + HEAD_V] = k.astype(BF16)
    v = z[:, 2 * ATTN_WIDTH:]
    v_ref[...] = v
    vb_ref[...] = v.astype(BF16)


def _qkv(x, g, w, cos, sin_signed, tm):
    m = x.shape[0]
    nt = cos.shape[0] // tm
    row = lambda i: (i, 0)
    return pl.pallas_call(
        _qkv_kernel,
        grid=(m // tm,),
        in_specs=[pl.BlockSpec((tm, D_MODEL), row),
                  _const_spec((1, D_MODEL)),
                  _col_block_spec(D_MODEL, *QKV_BLOCK),
                  pl.BlockSpec((tm, LANES), lambda i: (i % nt, 0)),
                  pl.BlockSpec((tm, LANES), lambda i: (i % nt, 0))],
        out_specs=[pl.BlockSpec((tm, D_MODEL), row),
                   pl.BlockSpec((2, tm, ATTN_WIDTH), lambda i: (0, i, 0)),
                   pl.BlockSpec((tm, ATTN_WIDTH), row), pl.BlockSpec((tm, ATTN_WIDTH), row),
                   pl.BlockSpec((tm, ATTN_WIDTH), row), pl.BlockSpec((tm, ATTN_WIDTH), row)],
        out_shape=[jax.ShapeDtypeStruct((m, D_MODEL), BF16),
                   jax.ShapeDtypeStruct((2, m, ATTN_WIDTH), BF16),
                   jax.ShapeDtypeStruct((m, ATTN_WIDTH), F32), jax.ShapeDtypeStruct((m, ATTN_WIDTH), BF16),
                   jax.ShapeDtypeStruct((m, ATTN_WIDTH), F32), jax.ShapeDtypeStruct((m, ATTN_WIDTH), BF16)],
        compiler_params=_cparams("parallel"),
        name="qkv_rope",
    )(x, g, w, cos, sin_signed)


HALO = 8


def _side_specs(side, steps):
    specs, shapes = [], []
    for a in side:
        rows = a.shape[0] // steps
        assert rows * steps == a.shape[0] and rows % 16 == 0
        specs.append(pl.BlockSpec((rows, a.shape[1]), lambda i: (i, 0)))
        shapes.append(jax.ShapeDtypeStruct(a.shape, BF16))
    return specs, shapes


def _side_cast(src_refs, dst_refs):
    for src, dst in zip(src_refs, dst_refs):
        dst[...] = src[...].astype(BF16)


def _conv_prompt_kernel(tiles_per_seq, n_side, h_ref, w_ref, wc_ref, *rest):
    side_in, (c_ref, st_ref), side_out, u_scr = rest[:n_side], rest[n_side:n_side + 2], rest[n_side + 2:-1], rest[-1]
    _side_cast(side_in, side_out)
    i = pl.program_id(0)
    tm = h_ref.shape[0]
    z = jnp.dot(h_ref[...], w_ref[...], preferred_element_type=F32)
    u = z[:, CONV_DIM:2 * CONV_DIM] * z[:, 2 * CONV_DIM:]

    @pl.when(i % tiles_per_seq == 0)
    def _():
        u_scr[0:HALO, :] = jnp.zeros((HALO, CONV_DIM), F32)

    @pl.when(i % tiles_per_seq != 0)
    def _():
        u_scr[0:HALO, :] = u_scr[tm:tm + HALO, :]

    u_scr[HALO:HALO + tm, :] = u
    wc = wc_ref[...]
    y = (wc[0:1, :] * u_scr[HALO - 2:HALO - 2 + tm, :]
         + wc[1:2, :] * u_scr[HALO - 1:HALO - 1 + tm, :]
         + wc[2:3, :] * u)
    c_ref[...] = (z[:, :CONV_DIM] * y).astype(BF16)
    st_ref[...] = u[tm - HALO:, :]


def _conv_prompt(h, w, w_conv, tm, seq, side=()):
    m = h.shape[0]
    tiles_per_seq = seq // tm
    side_specs, side_shapes = _side_specs(side, m // tm)
    return pl.pallas_call(
        functools.partial(_conv_prompt_kernel, tiles_per_seq, len(side)),
        grid=(m // tm,),
        in_specs=[pl.BlockSpec((tm, D_MODEL), lambda i: (i, 0)),
                  _col_block_spec(D_MODEL, *CONV_BLOCK),
                  _const_spec((CONV_WIDTH, CONV_DIM))] + side_specs,
        out_specs=[pl.BlockSpec((tm, CONV_DIM), lambda i: (i, 0)),
                   pl.BlockSpec((None, HALO, CONV_DIM), lambda i: (i // tiles_per_seq, 0, 0))] + side_specs,
        out_shape=[jax.ShapeDtypeStruct((m, CONV_DIM), BF16),
                   jax.ShapeDtypeStruct((m // seq, HALO, CONV_DIM), F32)] + side_shapes,
        scratch_shapes=[pltpu.VMEM((tm + HALO, CONV_DIM), F32)],
        compiler_params=_cparams("arbitrary"),
        name="conv_prompt",
    )(h, w, w_conv, *side)


def _conv_sample_kernel(h_ref, w_ref, wc_ref, st_ref, c_ref, nst_ref):
    z = jnp.dot(h_ref[...], w_ref[...], preferred_element_type=F32)
    u = z[:, CONV_DIM:2 * CONV_DIM] * z[:, 2 * CONV_DIM:]
    wc = wc_ref[...]
    s0 = st_ref[:, :CONV_DIM]
    s1 = st_ref[:, CONV_DIM:]
    y = wc[0:1, :] * s0 + wc[1:2, :] * s1 + wc[2:3, :] * u
    c_ref[...] = (z[:, :CONV_DIM] * y).astype(BF16)
    nst_ref[:, :CONV_DIM] = s1
    nst_ref[:, CONV_DIM:] = u


def _conv_sample(h, w, w_conv, state2d):
    m = h.shape[0]
    return pl.pallas_call(
        _conv_sample_kernel,
        grid=(1,),
        in_specs=[_const_spec((m, D_MODEL)), _col_block_spec(D_MODEL, *CONV_BLOCK),
                  _const_spec((CONV_WIDTH, CONV_DIM)), _const_spec((m, 2 * CONV_DIM))],
        out_specs=[pl.BlockSpec((m, CONV_DIM), lambda i: (0, 0)),
                   pl.BlockSpec((m, 2 * CONV_DIM), lambda i: (0, 0))],
        out_shape=[jax.ShapeDtypeStruct((m, CONV_DIM), BF16),
                   jax.ShapeDtypeStruct((m, 2 * CONV_DIM), F32)],
        compiler_params=_cparams("arbitrary"),
        name="conv_sample",
    )(h, w, w_conv, state2d)


def _gate_kernel(n_side, h_ref, wa_ref, wc_ref, *rest):
    side_in, (ga_ref, gc_ref), side_out = rest[:n_side], rest[n_side:n_side + 2], rest[n_side + 2:]
    _side_cast(side_in, side_out)
    h = h_ref[...]
    ga_ref[...] = jax.nn.sigmoid(jnp.dot(h, wa_ref[...], preferred_element_type=F32)).astype(BF16)
    gc_ref[...] = jax.nn.sigmoid(jnp.dot(h, wc_ref[...], preferred_element_type=F32)).astype(BF16)


def _gates(h, w, tm, side=()):
    m = h.shape[0]
    row = lambda i: (i, 0)
    side_specs, side_shapes = _side_specs(side, m // tm)
    return pl.pallas_call(
        functools.partial(_gate_kernel, len(side)),
        grid=(m // tm,),
        in_specs=[pl.BlockSpec((tm, D_MODEL), row),
                  _col_block_spec(D_MODEL, *GATE_A_BLOCK), _col_block_spec(D_MODEL, *GATE_C_BLOCK)] + side_specs,
        out_specs=[pl.BlockSpec((tm, D_MODEL), row), pl.BlockSpec((tm, D_MODEL), row)] + side_specs,
        out_shape=[jax.ShapeDtypeStruct((m, D_MODEL), BF16), jax.ShapeDtypeStruct((m, D_MODEL), BF16)] + side_shapes,
        compiler_params=_cparams("parallel"),
        name="gates",
    )(h, w, w, *side)


def _lam(lam_ref, lam_init):
    lv = lam_ref[...]
    a = jnp.sum(lv[0:1, :] * lv[1:2, :], axis=-1, keepdims=True)
    b = jnp.sum(lv[2:3, :] * lv[3:4, :], axis=-1, keepdims=True)
    return jnp.exp(a) - jnp.exp(b) + lam_init


def _subln(o, g, lam_init):
    return o * lax.rsqrt(jnp.mean(o * o, axis=-1, keepdims=True) + EPS) * g * (1.0 - lam_init)


ROW_BLOCK = 128


def _attn_tile(diag, qq_ref, k_ref, v_ref, m_scr, acc_scr):
    tq = qq_ref.shape[1]
    tk = k_ref.shape[0]
    k = k_ref[...]
    v_ext = jnp.concatenate([v_ref[...], jnp.ones((tk, HEAD_V), BF16)], axis=1)
    for mp in range(2):
        for r0 in range(0, tq, ROW_BLOCK):
            c0 = 0 if diag is None else diag * tk
            if r0 + ROW_BLOCK - 1 < c0:
                continue
            rows = pl.ds(mp * tq + r0, ROW_BLOCK)
            s = lax.dot_general(qq_ref[mp, r0:r0 + ROW_BLOCK, :], k, (((1,), (1,)), ((), ())),
                                preferred_element_type=F32)
            if diag is not None and r0 < c0 + tk - 1:
                qpos = r0 + lax.broadcasted_iota(jnp.int32, s.shape, 0)
                kpos = c0 + lax.broadcasted_iota(jnp.int32, s.shape, 1)
                s = jnp.where(kpos <= qpos, s, NEG_INF)
            m_prev = m_scr[rows, :]
            m_new = jnp.maximum(m_prev, jnp.max(s, axis=1, keepdims=True))
            p = jnp.exp2(s - jnp.tile(m_new, (1, tk // LANES)))
            alpha = jnp.exp2(m_prev - m_new)
            pv = jnp.dot(p.astype(BF16), v_ext, preferred_element_type=F32)
            acc_scr[rows, :] = jnp.tile(alpha, (1, 2)) * acc_scr[rows, :] + pv
            m_scr[rows, :] = m_new


def _attn_prompt_step(lam_init, i, j, qq_ref, k_ref, v_ref, lam_ref, g_ref, o_ref, m_scr, acc_scr):
    tq = qq_ref.shape[1]
    ratio = tq // k_ref.shape[0]

    @pl.when(j == 0)
    def _():
        m_scr[...] = jnp.full(m_scr.shape, NEG_INF, F32)
        acc_scr[...] = jnp.zeros(acc_scr.shape, F32)

    @pl.when(j < ratio * i)
    def _():
        _attn_tile(None, qq_ref, k_ref, v_ref, m_scr, acc_scr)

    for d in range(ratio):
        @pl.when(j == ratio * i + d)
        def _(d=d):
            _attn_tile(d, qq_ref, k_ref, v_ref, m_scr, acc_scr)

    @pl.when(j == ratio * i + ratio - 1)
    def _():
        acc = acc_scr[...]
        o12 = acc[:, :HEAD_V] / acc[:, HEAD_V:]
        o = o12[:tq] - _lam(lam_ref, lam_init) * o12[tq:]
        o_ref[...] = _subln(o, g_ref[...], lam_init).astype(BF16)


ROWS = 2 * ATTN_HEADS
PAGE_ROWS = PAGE_SIZE * ATTN_HEADS


def _attn_sample_step(lam_init, first, last, tail, qd_ref, kn_ref, vn_ref, lam_ref, g_ref, k_refs, v_refs, o_ref,
                      m_scr, l_scr, acc_scr):
    n = len(k_refs)
    qd = qd_ref[...]
    slot_bias = jnp.where(last, NEG_INF, 0.0)

    @pl.when(first)
    def _():
        m_scr[...] = jnp.sum(qd.astype(F32) * kn_ref[...], axis=-1, keepdims=True)
        l_scr[...] = jnp.ones(l_scr.shape, F32)
        acc_scr[...] = vn_ref[...]

    pieces = [lax.dot_general(qd, k_refs[p][...].astype(BF16), (((1,), (1,)), ((), ())),
                              preferred_element_type=F32) for p in range(n)]
    pieces = [sp if p < tail else sp + slot_bias for p, sp in enumerate(pieces)]
    s = jnp.concatenate(pieces, axis=1)
    col = lax.broadcasted_iota(jnp.int32, s.shape, 1)
    row = lax.broadcasted_iota(jnp.int32, s.shape, 0)
    s = jnp.where((col % ATTN_HEADS) == (row % ATTN_HEADS), s, NEG_INF)
    m_prev = m_scr[...]
    m_new = jnp.maximum(m_prev, jnp.max(s, axis=-1, keepdims=True))
    alpha = jnp.exp2(m_prev - m_new)
    p_all = jnp.exp2(s - m_new)
    l_scr[...] = alpha * l_scr[...] + jnp.sum(p_all, axis=-1, keepdims=True)
    pb = p_all.astype(BF16)
    pv = jnp.dot(pb[:, 0:PAGE_ROWS], v_refs[0][...].astype(BF16), preferred_element_type=F32)
    for p in range(1, n):
        pv += jnp.dot(pb[:, p * PAGE_ROWS:(p + 1) * PAGE_ROWS], v_refs[p][...].astype(BF16),
                      preferred_element_type=F32)
    acc_scr[...] = alpha * acc_scr[...] + pv
    m_scr[...] = m_new

    @pl.when(last)
    def _():
        o12 = acc_scr[...] / l_scr[...]
        o = o12[:ATTN_HEADS] - _lam(lam_ref, lam_init) * o12[ATTN_HEADS:]
        o_ref[...] = _subln(o, g_ref[...], lam_init).astype(BF16)


def _chunk_copies(start, pg_ref, sq_ref, ck_hbm, cv_hbm, kbuf, vbuf, sem, n, steps_per_seq, tail, chunk, slot):
    def run(lo, hi):
        for p in range(lo, hi):
            page = pg_ref[chunk * n + p]
            for src, dst, kind in ((ck_hbm, kbuf, 0), (cv_hbm, vbuf, 1)):
                copy = pltpu.make_async_copy(src.at[page], dst.at[slot, p], sem.at[kind, slot])
                if start:
                    copy.start()
                else:
                    copy.wait()

    run(0, tail)
    if tail < n:
        is_last = chunk - sq_ref[chunk] * steps_per_seq == steps_per_seq - 1
        pl.when(jnp.logical_not(is_last))(lambda: run(tail, n))


def _attn_kernel(lam_init, n, steps_per_seq, tail, it_ref, jt_ref, pg_ref, sq_ref,
                 qq_ref, k_ref, v_ref, lam_ref, g_ref, qd_ref, kn_ref, vn_ref, ck_hbm, cv_hbm,
                 o_ref, os_ref, m_scr, acc_scr, dm_scr, dl_scr, dacc_scr, kbuf, vbuf, sem):
    step = pl.program_id(2)
    steps = pl.num_programs(0) * pl.num_programs(1) * pl.num_programs(2)
    lin = (pl.program_id(0) * pl.num_programs(1) + pl.program_id(1)) * pl.num_programs(2) + step
    slot = lin % 2
    copies = functools.partial(_chunk_copies, pg_ref=pg_ref, sq_ref=sq_ref, ck_hbm=ck_hbm, cv_hbm=cv_hbm,
                               kbuf=kbuf, vbuf=vbuf, sem=sem, n=n, steps_per_seq=steps_per_seq, tail=tail)

    @pl.when(lin == 0)
    def _():
        copies(True, chunk=0, slot=0)

    @pl.when(lin + 1 < steps)
    def _():
        copies(True, chunk=lin + 1, slot=1 - slot)

    _attn_prompt_step(lam_init, it_ref[step], jt_ref[step], qq_ref, k_ref, v_ref, lam_ref, g_ref, o_ref,
                      m_scr, acc_scr)

    copies(False, chunk=lin, slot=slot)
    t = lin - sq_ref[lin] * steps_per_seq
    _attn_sample_step(lam_init, t == 0, t == steps_per_seq - 1, tail, qd_ref, kn_ref, vn_ref, lam_ref, g_ref,
                      [kbuf.at[slot, p] for p in range(n)], [vbuf.at[slot, p] for p in range(n)],
                      os_ref, dm_scr, dl_scr, dacc_scr)


def _attn(qq, kb, vb, page_table, qd, k_new, v_new, ck, cv, lam_vecs, g_subln, lam_init, batch, seq, tq, tk):
    m = kb.shape[0]
    nq = seq // tq
    nk = seq // tk
    ratio = tq // tk
    pairs = [(i, j) for i in range(nq) for j in range(ratio * (i + 1))]
    i_tab = jnp.array([p[0] for p in pairs], jnp.int32)
    j_tab = jnp.array([p[1] for p in pairs], jnp.int32)
    npairs = len(pairs)

    bd, n_pages = page_table.shape
    steps = batch * ATTN_HEADS * npairs
    assert steps % bd == 0
    steps_per_seq = steps // bd
    n = -(-n_pages // steps_per_seq)
    tail = n_pages - (steps_per_seq - 1) * n
    assert 0 < tail <= n
    seq_of_step = [t // steps_per_seq for t in range(steps)]
    slot_page = [[min((t % steps_per_seq) * n + p, n_pages - 1) for p in range(n)] for t in range(steps)]
    pages = page_table[jnp.array(seq_of_step, jnp.int32)[:, None], jnp.array(slot_page, jnp.int32)].reshape(steps * n)
    seq_tab = jnp.array(seq_of_step, jnp.int32)

    def lin(b, h, s):
        return (b * ATTN_HEADS + h) * npairs + s

    seq_row = lambda b, h, s, it, jt, pg, sq: (sq[lin(b, h, s)], 0, 0)
    fixed = lambda b, h, s, it, jt, pg, sq: (0, 0)
    grid_spec = pltpu.PrefetchScalarGridSpec(
        num_scalar_prefetch=4,
        grid=(batch, ATTN_HEADS, npairs),
        in_specs=[pl.BlockSpec((2, tq, HEAD_V), lambda b, h, s, it, jt, pg, sq: (0, b * nq + it[s], h)),
                  pl.BlockSpec((tk, HEAD_V), lambda b, h, s, it, jt, pg, sq: (b * nk + jt[s], h)),
                  pl.BlockSpec((tk, HEAD_V), lambda b, h, s, it, jt, pg, sq: (b * nk + jt[s], h)),
                  pl.BlockSpec((8, QK_DIM), fixed),
                  pl.BlockSpec((1, HEAD_V), fixed),
                  pl.BlockSpec((None, ROWS, HEAD_V), seq_row),
                  pl.BlockSpec((None, ROWS, HEAD_V), seq_row),
                  pl.BlockSpec((None, ROWS, HEAD_V), seq_row),
                  pl.BlockSpec(memory_space=pl.ANY), pl.BlockSpec(memory_space=pl.ANY)],
        out_specs=[pl.BlockSpec((tq, HEAD_V), lambda b, h, s, it, jt, pg, sq: (b * nq + it[s], h)),
                   pl.BlockSpec((None, ATTN_HEADS, HEAD_V), seq_row)],
        scratch_shapes=[pltpu.VMEM((2 * tq, LANES), F32), pltpu.VMEM((2 * tq, 2 * HEAD_V), F32),
                        pltpu.VMEM((ROWS, 1), F32), pltpu.VMEM((ROWS, 1), F32), pltpu.VMEM((ROWS, HEAD_V), F32),
                        pltpu.VMEM((2, n, PAGE_ROWS, HEAD_V), F32), pltpu.VMEM((2, n, PAGE_ROWS, HEAD_V), F32),
                        pltpu.SemaphoreType.DMA((2, 2))],
    )
    return pl.pallas_call(
        functools.partial(_attn_kernel, lam_init, n, steps_per_seq, tail),
        grid_spec=grid_spec,
        out_shape=[jax.ShapeDtypeStruct((m, ATTN_WIDTH), BF16),
                   jax.ShapeDtypeStruct((bd, ATTN_HEADS, HEAD_V), BF16)],
        compiler_params=_cparams("arbitrary", "arbitrary", "arbitrary"),
        name="attn",
    )(i_tab, j_tab, pages, seq_tab, qq, kb, vb, lam_vecs, g_subln, qd, k_new, v_new, ck, cv)


def _merge_kernel(o_ref, c_ref, ga_ref, gc_ref, x_ref, wa_ref, wc_ref, wo_ref, g_ref, x1_ref, h2_ref):
    a_out = jnp.dot(o_ref[...], wa_ref[...], preferred_element_type=F32)
    c_out = jnp.dot(c_ref[...], wc_ref[...], preferred_element_type=F32)
    merged = ga_ref[...].astype(F32) * a_out + gc_ref[...].astype(F32) * c_out
    x1 = x_ref[...] + jnp.dot(merged.astype(BF16), wo_ref[...], preferred_element_type=F32)
    x1_ref[...] = x1
    h2_ref[...] = _rms(x1, g_ref[...]).astype(BF16)


def _merge(o, c, ga, gc, x, wa, wc, wo, g_ffn, tm):
    m = x.shape[0]
    row = lambda i: (i, 0)
    return pl.pallas_call(
        _merge_kernel,
        grid=(m // tm,),
        in_specs=[pl.BlockSpec((tm, ATTN_WIDTH), row), pl.BlockSpec((tm, CONV_DIM), row),
                  pl.BlockSpec((tm, D_MODEL), row), pl.BlockSpec((tm, D_MODEL), row),
                  pl.BlockSpec((tm, D_MODEL), row),
                  _const_spec((ATTN_WIDTH, D_MODEL)), _const_spec((CONV_DIM, D_MODEL)),
                  _const_spec((D_MODEL, D_MODEL)), _const_spec((1, D_MODEL))],
        out_specs=[pl.BlockSpec((tm, D_MODEL), row), pl.BlockSpec((tm, D_MODEL), row)],
        out_shape=[jax.ShapeDtypeStruct((m, D_MODEL), F32), jax.ShapeDtypeStruct((m, D_MODEL), BF16)],
        compiler_params=_cparams("parallel"),
        name="merge_out",
    )(o, c, ga, gc, x, wa, wc, wo, g_ffn)


FF_BLOCK = 512


def _ffn_block(j, h_ref, x1_ref, wg_ref, wu_ref, wd_ref, gf_ref, y_ref, acc_scr):
    @pl.when(j == 0)
    def _():
        acc_scr[...] = jnp.zeros(acc_scr.shape, F32)

    h = h_ref[...]
    g = jnp.dot(h, wg_ref[...], preferred_element_type=F32)
    u = jnp.dot(h, wu_ref[...], preferred_element_type=F32)
    act = (g * jax.nn.sigmoid(g) * u).astype(BF16)
    acc_scr[...] += jnp.dot(act, wd_ref[...], preferred_element_type=F32)

    @pl.when(j == pl.num_programs(1) - 1)
    def _():
        y_ref[...] = _rms(x1_ref[...] + acc_scr[...], gf_ref[...])


def _ffn_kernel(h_ref, x1_ref, hs_ref, x1s_ref, wg_ref, wu_ref, wd_ref, gf_ref, y_ref, ys_ref, acc_scr, accs_scr):
    i = pl.program_id(0)
    j = pl.program_id(1)
    _ffn_block(j, h_ref, x1_ref, wg_ref, wu_ref, wd_ref, gf_ref, y_ref, acc_scr)

    @pl.when(i == 0)
    def _():
        _ffn_block(j, hs_ref, x1s_ref, wg_ref, wu_ref, wd_ref, gf_ref, ys_ref, accs_scr)


def _ffn(h2, x1, h2_s, x1_s, w_gu, w_down, g_final, tm):
    m = h2.shape[0]
    ms = h2_s.shape[0]
    nj = D_FF // FF_BLOCK
    row = lambda i, j: (i, 0)
    whole = lambda i, j: (0, 0)
    return pl.pallas_call(
        _ffn_kernel,
        grid=(m // tm, nj),
        in_specs=[pl.BlockSpec((tm, D_MODEL), row), pl.BlockSpec((tm, D_MODEL), row),
                  pl.BlockSpec((ms, D_MODEL), whole), pl.BlockSpec((ms, D_MODEL), whole),
                  pl.BlockSpec((D_MODEL, FF_BLOCK), lambda i, j: (0, j)),
                  pl.BlockSpec((D_MODEL, FF_BLOCK), lambda i, j: (0, nj + j)),
                  pl.BlockSpec((FF_BLOCK, D_MODEL), lambda i, j: (j, 0)),
                  pl.BlockSpec((1, D_MODEL), whole)],
        out_specs=[pl.BlockSpec((tm, D_MODEL), row), pl.BlockSpec((ms, D_MODEL), whole)],
        out_shape=[jax.ShapeDtypeStruct((m, D_MODEL), F32), jax.ShapeDtypeStruct((ms, D_MODEL), F32)],
        scratch_shapes=[pltpu.VMEM((tm, D_MODEL), F32), pltpu.VMEM((ms, D_MODEL), F32)],
        compiler_params=_cparams("arbitrary", "arbitrary"),
        name="ffn",
    )(h2, x1, h2_s, x1_s, w_gu, w_gu, w_down, g_final)


def _rope_tables(pos):
    half = QK_DIM // 2
    inv = ROPE_THETA ** (-jnp.arange(half, dtype=F32) / half)
    ang = pos.astype(F32)[:, None] * inv[None, :]
    cos = jnp.tile(jnp.cos(ang), (1, LANES // half))
    sin = jnp.tile(jnp.sin(ang), (1, LANES // half))
    first_half = (jnp.arange(LANES) & half) == 0
    return cos, jnp.where(first_half[None, :], -sin, sin)


def kernel(x_prompt, x_sample, cache_k, cache_v, state_conv, page_table, w_in, w_conv, lambda_q1, lambda_k1, lambda_q2, lambda_k2, g_subln, w_branch_attn, w_branch_conv, w_out, g_mix, g_ffn, w_gate_up, w_down, g_final):
    bp, sp, _ = x_prompt.shape
    bd, sd, _ = x_sample.shape
    assert sd == 1 and w_in.shape[0] == 1
    n_pool = cache_k.shape[1]
    past = page_table.shape[1] * PAGE_SIZE
    lam_init = 0.8 - 0.6 * math.exp(-0.3 * 0)

    w_qkv = w_cv = w_gt = w_in[0].astype(BF16)
    g_mix2 = g_mix[0].reshape(1, D_MODEL)
    g_ffn2 = g_ffn[0].reshape(1, D_MODEL)
    g_fin2 = g_final.reshape(1, D_MODEL)
    g_sub2 = g_subln[0].reshape(1, HEAD_V)
    lam_vecs = jnp.concatenate([lambda_q1, lambda_k1, lambda_q2, lambda_k2, jnp.zeros((4, QK_DIM), F32)], axis=0)
    wcv3 = w_conv[0]

    mp = bp * sp
    xp = x_prompt.reshape(mp, D_MODEL)
    cos_p, sin_p = _rope_tables(jnp.arange(sp, dtype=jnp.int32))
    tm = 512
    hp, qq, k_p, kb, v_p, vb = _qkv(xp, g_mix2, w_qkv, cos_p, sin_p, tm)
    c_p, st_p, w_gu = _conv_prompt(hp, w_cv, wcv3, tm, sp, side=(w_gate_up[0],))
    ga_p, gc_p, w_dn, wo, wa, wc = _gates(hp, w_gt, tm,
                                          side=(w_down[0], w_out[0], w_branch_attn[0], w_branch_conv[0]))

    xs = x_sample.reshape(bd, D_MODEL)
    cos_s, sin_s = _rope_tables(jnp.full((bd,), past, dtype=jnp.int32))
    hs, qq_s, k_s, _, v_s, _ = _qkv(xs, g_mix2, w_qkv, cos_s, sin_s, bd)
    c_s, st_s = _conv_sample(hs, w_cv, wcv3, state_conv[0].reshape(bd, 2 * CONV_DIM))
    ga_s, gc_s = _gates(hs, w_gt, bd)[:2]
    qd = jnp.transpose(qq_s.reshape(2, bd, ATTN_HEADS, HEAD_V), (1, 0, 2, 3)).reshape(bd, ROWS, HEAD_V)
    k_new = jnp.tile(k_s.reshape(bd, ATTN_HEADS, HEAD_V), (1, 2, 1))
    v_new = jnp.tile(v_s.reshape(bd, ATTN_HEADS, HEAD_V), (1, 2, 1))

    o_p, o_s = _attn(qq, kb, vb, page_table, qd, k_new, v_new,
                     cache_k.reshape(n_pool, PAGE_ROWS, HEAD_V), cache_v.reshape(n_pool, PAGE_ROWS, HEAD_V),
                     lam_vecs, g_sub2, lam_init, bp, sp, 1024, 512)
    x1_p, h2_p = _merge(o_p, c_p, ga_p, gc_p, xp, wa, wc, wo, g_ffn2, 256)
    x1_s, h2_s = _merge(o_s.reshape(bd, ATTN_WIDTH), c_s, ga_s, gc_s, xs, wa, wc, wo, g_ffn2, bd)
    y_p, y_s = _ffn(h2_p, x1_p, h2_s, x1_s, w_gu, w_dn, g_fin2, tm)

    return (y_p.reshape(bp, sp, D_MODEL),
            y_s.reshape(bd, sd, D_MODEL),
            k_p.reshape(1, bp, sp, ATTN_HEADS, HEAD_V),
            v_p.reshape(1, bp, sp, ATTN_HEADS, HEAD_V),
            st_p[:, HALO - 2:, :].reshape(1, bp, CONV_WIDTH - 1, CONV_DIM),
            k_s.reshape(1, bd, sd, ATTN_HEADS, HEAD_V),
            v_s.reshape(1, bd, sd, ATTN_HEADS, HEAD_V),
            st_s.reshape(1, bd, CONV_WIDTH - 1, CONV_DIM))
```

```python
import functools
import math

import jax
import jax.numpy as jnp
from jax import lax
from jax.experimental import pallas as pl
from jax.experimental.pallas import tpu as pltpu

F32 = jnp.float32
BF16 = jnp.bfloat16

D_MODEL = 2048
ATTN_HEADS = 8
QK_DIM = 64
HEAD_V = 128
ATTN_WIDTH = ATTN_HEADS * HEAD_V
CONV_DIM = D_MODEL // 2
CONV_WIDTH = 3
D_FF = 5632
ROPE_THETA = 10000.0
PAGE_SIZE = 128
EPS = 1e-5
SCALE = QK_DIM ** -0.5
Q_SCALE = SCALE * math.log2(math.e)
NEG_INF = -1e30
LANES = 128

VMEM_LIMIT = 56 * 1024 * 1024


def _cparams(*sem):
    return pltpu.CompilerParams(dimension_semantics=sem, vmem_limit_bytes=VMEM_LIMIT)


def _const_spec(shape):
    nd = len(shape)
    return pl.BlockSpec(shape, lambda *_: (0,) * nd, pipeline_mode=pl.Buffered(1))


def _col_block_spec(rows, width, block):
    return pl.BlockSpec((rows, width), lambda *_: (0, block), pipeline_mode=pl.Buffered(1))


QKV_BLOCK = (3 * ATTN_WIDTH, 0)
CONV_BLOCK = (3 * CONV_DIM, 1)
GATE_A_BLOCK = (D_MODEL, 3)
GATE_C_BLOCK = (D_MODEL, 4)
assert QKV_BLOCK[0] == CONV_BLOCK[0] and GATE_A_BLOCK[1] * D_MODEL == QKV_BLOCK[0] + CONV_BLOCK[0]


def _rms(x, g):
    return x * lax.rsqrt(jnp.mean(x * x, axis=-1, keepdims=True) + EPS) * g


def _rope(z, cos, sin_signed, first_half):
    partner = jnp.where(first_half, pltpu.roll(z, 96, 1), pltpu.roll(z, 32, 1))
    return z * cos + partner * sin_signed


def _on_first_tile_too(i, run):
    pl.when(i == 0)(functools.partial(run, True))
    pl.when(i != 0)(functools.partial(run, False))


def _rows(with_sample, tile_ref, sample_ref):
    return jnp.concatenate([tile_ref[...], sample_ref[...]], axis=0) if with_sample else tile_ref[...]


def _whole(shape):
    nd = len(shape)
    return pl.BlockSpec(shape, lambda *_: (0,) * nd)


def _qkv_kernel(x_ref, xs_ref, g_ref, w_ref, cos_ref, sin_ref, coss_ref, sins_ref,
                h_ref, qq_ref, k_ref, kb_ref, v_ref, vb_ref, hs_ref, qqs_ref, ks_ref, vs_ref):
    tm = x_ref.shape[0]

    def run(with_sample):
        h = _rms(_rows(with_sample, x_ref, xs_ref), g_ref[...]).astype(BF16)
        z = jnp.dot(h, w_ref[...], preferred_element_type=F32)
        cos = _rows(with_sample, cos_ref, coss_ref)
        sin = _rows(with_sample, sin_ref, sins_ref)
        lane = lax.broadcasted_iota(jnp.int32, cos.shape, 1)
        first_half = (lane & 32) == 0
        map1 = lane < QK_DIM
        h_ref[...] = h[:tm]
        if with_sample:
            hs_ref[...] = h[tm:]
        for hh in range(ATTN_HEADS):
            cols = slice(hh * HEAD_V, (hh + 1) * HEAD_V)
            q = _rope(z[:, cols], cos, sin, first_half) * Q_SCALE
            q1 = jnp.where(map1, q, 0.0).astype(BF16)
            q2 = jnp.where(map1, 0.0, q).astype(BF16)
            k = _rope(z[:, ATTN_WIDTH + hh * HEAD_V:ATTN_WIDTH + (hh + 1) * HEAD_V], cos, sin, first_half)
            qq_ref[0, :, cols] = q1[:tm]
            qq_ref[1, :, cols] = q2[:tm]
            k_ref[:, cols] = k[:tm]
            kb_ref[:, cols] = k[:tm].astype(BF16)
            if with_sample:
                qqs_ref[0, :, cols] = q1[tm:]
                qqs_ref[1, :, cols] = q2[tm:]
                ks_ref[:, cols] = k[tm:]
        v = z[:, 2 * ATTN_WIDTH:]
        v_ref[...] = v[:tm]
        vb_ref[...] = v[:tm].astype(BF16)
        if with_sample:
            vs_ref[...] = v[tm:]

    _on_first_tile_too(pl.program_id(0), run)


def _qkv(x, xs, g, w, cos, sin_signed, cos_s, sin_s, tm):
    m = x.shape[0]
    ms = xs.shape[0]
    nt = cos.shape[0] // tm
    row = lambda i: (i, 0)
    return pl.pallas_call(
        _qkv_kernel,
        grid=(m // tm,),
        in_specs=[pl.BlockSpec((tm, D_MODEL), row), _whole((ms, D_MODEL)),
                  _const_spec((1, D_MODEL)),
                  _col_block_spec(D_MODEL, *QKV_BLOCK),
                  pl.BlockSpec((tm, LANES), lambda i: (i % nt, 0)),
                  pl.BlockSpec((tm, LANES), lambda i: (i % nt, 0)),
                  _whole((ms, LANES)), _whole((ms, LANES))],
        out_specs=[pl.BlockSpec((tm, D_MODEL), row),
                   pl.BlockSpec((2, tm, ATTN_WIDTH), lambda i: (0, i, 0)),
                   pl.BlockSpec((tm, ATTN_WIDTH), row), pl.BlockSpec((tm, ATTN_WIDTH), row),
                   pl.BlockSpec((tm, ATTN_WIDTH), row), pl.BlockSpec((tm, ATTN_WIDTH), row),
                   _whole((ms, D_MODEL)), _whole((2, ms, ATTN_WIDTH)),
                   _whole((ms, ATTN_WIDTH)), _whole((ms, ATTN_WIDTH))],
        out_shape=[jax.ShapeDtypeStruct((m, D_MODEL), BF16),
                   jax.ShapeDtypeStruct((2, m, ATTN_WIDTH), BF16),
                   jax.ShapeDtypeStruct((m, ATTN_WIDTH), F32), jax.ShapeDtypeStruct((m, ATTN_WIDTH), BF16),
                   jax.ShapeDtypeStruct((m, ATTN_WIDTH), F32), jax.ShapeDtypeStruct((m, ATTN_WIDTH), BF16),
                   jax.ShapeDtypeStruct((ms, D_MODEL), BF16), jax.ShapeDtypeStruct((2, ms, ATTN_WIDTH), BF16),
                   jax.ShapeDtypeStruct((ms, ATTN_WIDTH), F32), jax.ShapeDtypeStruct((ms, ATTN_WIDTH), F32)],
        compiler_params=_cparams("arbitrary"),
        name="qkv_rope",
    )(x, xs, g, w, cos, sin_signed, cos_s, sin_s)


HALO = 8
CONV_CHUNK = 256


def _side_specs(side, steps):
    specs, shapes = [], []
    for a in side:
        rows = a.shape[0] // steps
        assert rows * steps == a.shape[0] and rows % 16 == 0
        specs.append(pl.BlockSpec((rows, a.shape[1]), lambda i: (i, 0)))
        shapes.append(jax.ShapeDtypeStruct(a.shape, BF16))
    return specs, shapes


def _side_cast(src_refs, dst_refs):
    for src, dst in zip(src_refs, dst_refs):
        dst[...] = src[...].astype(BF16)


def _conv_kernel(tiles_per_seq, n_side, h_ref, hs_ref, sst_ref, w_ref, wc_ref, *rest):
    side_in, side_out, u_scr = rest[:n_side], rest[n_side + 4:-1], rest[-1]
    c_ref, st_ref, cs_ref, nst_ref = rest[n_side:n_side + 4]
    _side_cast(side_in, side_out)
    i = pl.program_id(0)
    tm = h_ref.shape[0]

    @pl.when(i % tiles_per_seq == 0)
    def _():
        u_scr[0:HALO, :] = jnp.zeros((HALO, CONV_DIM), F32)

    @pl.when(i % tiles_per_seq != 0)
    def _():
        u_scr[0:HALO, :] = u_scr[tm:tm + HALO, :]

    def run(with_sample):
        h = _rows(with_sample, h_ref, hs_ref)
        wc = wc_ref[...]
        for c0 in range(0, CONV_DIM, CONV_CHUNK):
            ch = slice(c0, c0 + CONV_CHUNK)
            proj = lambda part: jnp.dot(h, w_ref[:, part * CONV_DIM + c0:part * CONV_DIM + c0 + CONV_CHUNK],
                                        preferred_element_type=F32)
            gate = proj(0)
            u_all = proj(1) * proj(2)
            u = u_all[:tm]
            u_scr[HALO:HALO + tm, ch] = u
            y = (wc[0:1, ch] * u_scr[HALO - 2:HALO - 2 + tm, ch]
                 + wc[1:2, ch] * u_scr[HALO - 1:HALO - 1 + tm, ch]
                 + wc[2:3, ch] * u)
            c_ref[:, ch] = (gate[:tm] * y).astype(BF16)
            st_ref[:, ch] = u[tm - HALO:, :]
            if with_sample:
                us = u_all[tm:]
                s0 = sst_ref[:, ch]
                s1 = sst_ref[:, CONV_DIM + c0:CONV_DIM + c0 + CONV_CHUNK]
                ys = wc[0:1, ch] * s0 + wc[1:2, ch] * s1 + wc[2:3, ch] * us
                cs_ref[:, ch] = (gate[tm:] * ys).astype(BF16)
                nst_ref[:, ch] = s1
                nst_ref[:, CONV_DIM + c0:CONV_DIM + c0 + CONV_CHUNK] = us

    _on_first_tile_too(i, run)


def _conv(h, hs, state2d, w, w_conv, tm, seq, side=()):
    m = h.shape[0]
    ms = hs.shape[0]
    tiles_per_seq = seq // tm
    side_specs, side_shapes = _side_specs(side, m // tm)
    return pl.pallas_call(
        functools.partial(_conv_kernel, tiles_per_seq, len(side)),
        grid=(m // tm,),
        in_specs=[pl.BlockSpec((tm, D_MODEL), lambda i: (i, 0)), _whole((ms, D_MODEL)), _whole((ms, 2 * CONV_DIM)),
                  _col_block_spec(D_MODEL, *CONV_BLOCK),
                  _const_spec((CONV_WIDTH, CONV_DIM))] + side_specs,
        out_specs=[pl.BlockSpec((tm, CONV_DIM), lambda i: (i, 0)),
                   pl.BlockSpec((None, HALO, CONV_DIM), lambda i: (i // tiles_per_seq, 0, 0)),
                   _whole((ms, CONV_DIM)), _whole((ms, 2 * CONV_DIM))] + side_specs,
        out_shape=[jax.ShapeDtypeStruct((m, CONV_DIM), BF16),
                   jax.ShapeDtypeStruct((m // seq, HALO, CONV_DIM), F32),
                   jax.ShapeDtypeStruct((ms, CONV_DIM), BF16),
                   jax.ShapeDtypeStruct((ms, 2 * CONV_DIM), F32)] + side_shapes,
        scratch_shapes=[pltpu.VMEM((tm + HALO, CONV_DIM), F32)],
        compiler_params=_cparams("arbitrary"),
        name="conv",
    )(h, hs, state2d, w, w_conv, *side)


def _gate_kernel(n_side, h_ref, hs_ref, wa_ref, wc_ref, *rest):
    side_in, side_out = rest[:n_side], rest[n_side + 4:]
    ga_ref, gc_ref, gas_ref, gcs_ref = rest[n_side:n_side + 4]
    _side_cast(side_in, side_out)
    tm = h_ref.shape[0]

    def run(with_sample):
        h = _rows(with_sample, h_ref, hs_ref)
        for w_ref, tile_out, sample_out in ((wa_ref, ga_ref, gas_ref), (wc_ref, gc_ref, gcs_ref)):
            gate = jax.nn.sigmoid(jnp.dot(h, w_ref[...], preferred_element_type=F32)).astype(BF16)
            tile_out[...] = gate[:tm]
            if with_sample:
                sample_out[...] = gate[tm:]

    _on_first_tile_too(pl.program_id(0), run)


def _gates(h, hs, w, tm, side=()):
    m = h.shape[0]
    ms = hs.shape[0]
    row = lambda i: (i, 0)
    side_specs, side_shapes = _side_specs(side, m // tm)
    return pl.pallas_call(
        functools.partial(_gate_kernel, len(side)),
        grid=(m // tm,),
        in_specs=[pl.BlockSpec((tm, D_MODEL), row), _whole((ms, D_MODEL)),
                  _col_block_spec(D_MODEL, *GATE_A_BLOCK), _col_block_spec(D_MODEL, *GATE_C_BLOCK)] + side_specs,
        out_specs=[pl.BlockSpec((tm, D_MODEL), row), pl.BlockSpec((tm, D_MODEL), row),
                   _whole((ms, D_MODEL)), _whole((ms, D_MODEL))] + side_specs,
        out_shape=[jax.ShapeDtypeStruct((m, D_MODEL), BF16), jax.ShapeDtypeStruct((m, D_MODEL), BF16),
                   jax.ShapeDtypeStruct((ms, D_MODEL), BF16), jax.ShapeDtypeStruct((ms, D_MODEL), BF16)] + side_shapes,
        compiler_params=_cparams("arbitrary"),
        name="gates",
    )(h, hs, w, w, *side)


def _lam(lam_ref, lam_init):
    lv = lam_ref[...]
    a = jnp.sum(lv[0:1, :] * lv[1:2, :], axis=-1, keepdims=True)
    b = jnp.sum(lv[2:3, :] * lv[3:4, :], axis=-1, keepdims=True)
    return jnp.exp(a) - jnp.exp(b) + lam_init


def _subln(o, g, lam_init):
    return o * lax.rsqrt(jnp.mean(o * o, axis=-1, keepdims=True) + EPS) * g * (1.0 - lam_init)


ROW_BLOCK = 128


def _attn_tile(diag, qq_ref, k_ref, v_ref, m_scr, acc_scr):
    tq = qq_ref.shape[1]
    tk = k_ref.shape[0]
    k = k_ref[...]
    v_ext = jnp.concatenate([v_ref[...], jnp.ones((tk, HEAD_V), BF16)], axis=1)
    for mp in range(2):
        for r0 in range(0, tq, ROW_BLOCK):
            c0 = 0 if diag is None else diag * tk
            if r0 + ROW_BLOCK - 1 < c0:
                continue
            rows = pl.ds(mp * tq + r0, ROW_BLOCK)
            s = lax.dot_general(qq_ref[mp, r0:r0 + ROW_BLOCK, :], k, (((1,), (1,)), ((), ())),
                                preferred_element_type=F32)
            if diag is not None and r0 < c0 + tk - 1:
                qpos = r0 + lax.broadcasted_iota(jnp.int32, s.shape, 0)
                kpos = c0 + lax.broadcasted_iota(jnp.int32, s.shape, 1)
                s = jnp.where(kpos <= qpos, s, NEG_INF)
            m_prev = m_scr[rows, :]
            m_new = jnp.maximum(m_prev, jnp.max(s, axis=1, keepdims=True))
            p = jnp.exp2(s - jnp.tile(m_new, (1, tk // LANES)))
            alpha = jnp.exp2(m_prev - m_new)
            pv = jnp.dot(p.astype(BF16), v_ext, preferred_element_type=F32)
            acc_scr[rows, :] = jnp.tile(alpha, (1, 2)) * acc_scr[rows, :] + pv
            m_scr[rows, :] = m_new


def _attn_prompt_step(lam_init, i, j, qq_ref, k_ref, v_ref, lam_ref, g_ref, o_ref, m_scr, acc_scr):
    tq = qq_ref.shape[1]
    ratio = tq // k_ref.shape[0]

    @pl.when(j == 0)
    def _():
        m_scr[...] = jnp.full(m_scr.shape, NEG_INF, F32)
        acc_scr[...] = jnp.zeros(acc_scr.shape, F32)

    @pl.when(j < ratio * i)
    def _():
        _attn_tile(None, qq_ref, k_ref, v_ref, m_scr, acc_scr)

    for d in range(ratio):
        @pl.when(j == ratio * i + d)
        def _(d=d):
            _attn_tile(d, qq_ref, k_ref, v_ref, m_scr, acc_scr)

    @pl.when(j == ratio * i + ratio - 1)
    def _():
        acc = acc_scr[...]
        o12 = acc[:, :HEAD_V] / acc[:, HEAD_V:]
        o = o12[:tq] - _lam(lam_ref, lam_init) * o12[tq:]
        o_ref[...] = _subln(o, g_ref[...], lam_init).astype(BF16)


ROWS = 2 * ATTN_HEADS
PAGE_ROWS = PAGE_SIZE * ATTN_HEADS


def _attn_sample_step(lam_init, first, last, tail, qd_ref, kn_ref, vn_ref, hb_ref, lam_ref, g_ref, k_refs, v_refs,
                      o_ref, m_scr, l_scr, acc_scr):
    n = len(k_refs)
    qd = qd_ref[...]
    slot_bias = jnp.where(last, NEG_INF, 0.0)

    @pl.when(first)
    def _():
        m_scr[...] = jnp.sum(qd.astype(F32) * kn_ref[...], axis=-1, keepdims=True)
        l_scr[...] = jnp.ones(l_scr.shape, F32)
        acc_scr[...] = vn_ref[...]

    head_bias = hb_ref[...]
    pieces = [lax.dot_general(qd, k_refs[p][...].astype(BF16), (((1,), (1,)), ((), ())),
                              preferred_element_type=F32) + head_bias for p in range(n)]
    pieces = [sp if p < tail else sp + slot_bias for p, sp in enumerate(pieces)]
    s = jnp.concatenate(pieces, axis=1)
    m_prev = m_scr[...]
    m_new = jnp.maximum(m_prev, jnp.max(s, axis=-1, keepdims=True))
    alpha = jnp.exp2(m_prev - m_new)
    p_all = jnp.exp2(s - m_new)
    l_scr[...] = alpha * l_scr[...] + jnp.sum(p_all, axis=-1, keepdims=True)
    pb = p_all.astype(BF16)
    pv = jnp.dot(pb[:, 0:PAGE_ROWS], v_refs[0][...].astype(BF16), preferred_element_type=F32)
    for p in range(1, n):
        pv += jnp.dot(pb[:, p * PAGE_ROWS:(p + 1) * PAGE_ROWS], v_refs[p][...].astype(BF16),
                      preferred_element_type=F32)
    acc_scr[...] = alpha * acc_scr[...] + pv
    m_scr[...] = m_new

    @pl.when(last)
    def _():
        o12 = acc_scr[...] / l_scr[...]
        o = o12[:ATTN_HEADS] - _lam(lam_ref, lam_init) * o12[ATTN_HEADS:]
        o_ref[...] = _subln(o, g_ref[...], lam_init).astype(BF16)


def _chunk_copies(start, pg_ref, sq_ref, ck_hbm, cv_hbm, kbuf, vbuf, sem, n, steps_per_seq, tail, chunk, slot):
    def run(lo, hi):
        for p in range(lo, hi):
            page = pg_ref[chunk * n + p]
            for src, dst, kind in ((ck_hbm, kbuf, 0), (cv_hbm, vbuf, 1)):
                copy = pltpu.make_async_copy(src.at[page], dst.at[slot, p], sem.at[kind, slot])
                if start:
                    copy.start()
                else:
                    copy.wait()

    run(0, tail)
    if tail < n:
        is_last = chunk - sq_ref[chunk] * steps_per_seq == steps_per_seq - 1
        pl.when(jnp.logical_not(is_last))(lambda: run(tail, n))


def _attn_kernel(lam_init, n, steps_per_seq, tail, it_ref, jt_ref, pg_ref, sq_ref,
                 qq_ref, k_ref, v_ref, lam_ref, g_ref, qd_ref, kn_ref, vn_ref, hb_ref, ck_hbm, cv_hbm,
                 o_ref, os_ref, m_scr, acc_scr, dm_scr, dl_scr, dacc_scr, kbuf, vbuf, sem):
    step = pl.program_id(2)
    steps = pl.num_programs(0) * pl.num_programs(1) * pl.num_programs(2)
    lin = (pl.program_id(0) * pl.num_programs(1) + pl.program_id(1)) * pl.num_programs(2) + step
    slot = lin % 2
    copies = functools.partial(_chunk_copies, pg_ref=pg_ref, sq_ref=sq_ref, ck_hbm=ck_hbm, cv_hbm=cv_hbm,
                               kbuf=kbuf, vbuf=vbuf, sem=sem, n=n, steps_per_seq=steps_per_seq, tail=tail)

    @pl.when(lin == 0)
    def _():
        copies(True, chunk=0, slot=0)

    @pl.when(lin + 1 < steps)
    def _():
        copies(True, chunk=lin + 1, slot=1 - slot)

    _attn_prompt_step(lam_init, it_ref[step], jt_ref[step], qq_ref, k_ref, v_ref, lam_ref, g_ref, o_ref,
                      m_scr, acc_scr)

    copies(False, chunk=lin, slot=slot)
    t = lin - sq_ref[lin] * steps_per_seq
    _attn_sample_step(lam_init, t == 0, t == steps_per_seq - 1, tail, qd_ref, kn_ref, vn_ref, hb_ref, lam_ref, g_ref,
                      [kbuf.at[slot, p] for p in range(n)], [vbuf.at[slot, p] for p in range(n)],
                      os_ref, dm_scr, dl_scr, dacc_scr)


def _attn(qq, kb, vb, page_table, qd, k_new, v_new, ck, cv, lam_vecs, g_subln, lam_init, batch, seq, tq, tk):
    m = kb.shape[0]
    nq = seq // tq
    nk = seq // tk
    ratio = tq // tk
    pairs = [(i, j) for i in range(nq) for j in range(ratio * (i + 1))]
    i_tab = jnp.array([p[0] for p in pairs], jnp.int32)
    j_tab = jnp.array([p[1] for p in pairs], jnp.int32)
    npairs = len(pairs)

    bd, n_pages = page_table.shape
    steps = batch * ATTN_HEADS * npairs
    assert steps % bd == 0
    steps_per_seq = steps // bd
    n = -(-n_pages // steps_per_seq)
    tail = n_pages - (steps_per_seq - 1) * n
    assert 0 < tail <= n
    seq_of_step = [t // steps_per_seq for t in range(steps)]
    slot_page = [[min((t % steps_per_seq) * n + p, n_pages - 1) for p in range(n)] for t in range(steps)]
    pages = page_table[jnp.array(seq_of_step, jnp.int32)[:, None], jnp.array(slot_page, jnp.int32)].reshape(steps * n)
    seq_tab = jnp.array(seq_of_step, jnp.int32)
    head_bias = jnp.where(jnp.arange(PAGE_ROWS)[None, :] % ATTN_HEADS == jnp.arange(ROWS)[:, None] % ATTN_HEADS,
                          0.0, NEG_INF).astype(F32)

    def lin(b, h, s):
        return (b * ATTN_HEADS + h) * npairs + s

    seq_row = lambda b, h, s, it, jt, pg, sq: (sq[lin(b, h, s)], 0, 0)
    fixed = lambda b, h, s, it, jt, pg, sq: (0, 0)
    grid_spec = pltpu.PrefetchScalarGridSpec(
        num_scalar_prefetch=4,
        grid=(batch, ATTN_HEADS, npairs),
        in_specs=[pl.BlockSpec((2, tq, HEAD_V), lambda b, h, s, it, jt, pg, sq: (0, b * nq + it[s], h)),
                  pl.BlockSpec((tk, HEAD_V), lambda b, h, s, it, jt, pg, sq: (b * nk + jt[s], h)),
                  pl.BlockSpec((tk, HEAD_V), lambda b, h, s, it, jt, pg, sq: (b * nk + jt[s], h)),
                  pl.BlockSpec((8, QK_DIM), fixed),
                  pl.BlockSpec((1, HEAD_V), fixed),
                  pl.BlockSpec((None, ROWS, HEAD_V), seq_row),
                  pl.BlockSpec((None, ROWS, HEAD_V), seq_row),
                  pl.BlockSpec((None, ROWS, HEAD_V), seq_row),
                  pl.BlockSpec((ROWS, PAGE_ROWS), fixed),
                  pl.BlockSpec(memory_space=pl.ANY), pl.BlockSpec(memory_space=pl.ANY)],
        out_specs=[pl.BlockSpec((tq, HEAD_V), lambda b, h, s, it, jt, pg, sq: (b * nq + it[s], h)),
                   pl.BlockSpec((None, ATTN_HEADS, HEAD_V), seq_row)],
        scratch_shapes=[pltpu.VMEM((2 * tq, LANES), F32), pltpu.VMEM((2 * tq, 2 * HEAD_V), F32),
                        pltpu.VMEM((ROWS, 1), F32), pltpu.VMEM((ROWS, 1), F32), pltpu.VMEM((ROWS, HEAD_V), F32),
                        pltpu.VMEM((2, n, PAGE_ROWS, HEAD_V), F32), pltpu.VMEM((2, n, PAGE_ROWS, HEAD_V), F32),
                        pltpu.SemaphoreType.DMA((2, 2))],
    )
    return pl.pallas_call(
        functools.partial(_attn_kernel, lam_init, n, steps_per_seq, tail),
        grid_spec=grid_spec,
        out_shape=[jax.ShapeDtypeStruct((m, ATTN_WIDTH), BF16),
                   jax.ShapeDtypeStruct((bd, ATTN_HEADS, HEAD_V), BF16)],
        compiler_params=_cparams("arbitrary", "arbitrary", "arbitrary"),
        name="attn",
    )(i_tab, j_tab, pages, seq_tab, qq, kb, vb, lam_vecs, g_subln, qd, k_new, v_new, head_bias, ck, cv)


def _merge_kernel(o_ref, c_ref, ga_ref, gc_ref, x_ref, os_ref, cs_ref, gas_ref, gcs_ref, xs_ref,
                  wa_ref, wc_ref, wo_ref, g_ref, x1_ref, h2_ref, x1s_ref, h2s_ref):
    tm = x_ref.shape[0]

    def run(with_sample):
        rows = functools.partial(_rows, with_sample)
        a_out = jnp.dot(rows(o_ref, os_ref), wa_ref[...], preferred_element_type=F32)
        c_out = jnp.dot(rows(c_ref, cs_ref), wc_ref[...], preferred_element_type=F32)
        merged = rows(ga_ref, gas_ref).astype(F32) * a_out + rows(gc_ref, gcs_ref).astype(F32) * c_out
        x1 = rows(x_ref, xs_ref) + jnp.dot(merged.astype(BF16), wo_ref[...], preferred_element_type=F32)
        h2 = _rms(x1, g_ref[...]).astype(BF16)
        x1_ref[...] = x1[:tm]
        h2_ref[...] = h2[:tm]
        if with_sample:
            x1s_ref[...] = x1[tm:]
            h2s_ref[...] = h2[tm:]

    _on_first_tile_too(pl.program_id(0), run)


def _merge(o, c, ga, gc, x, o_s, c_s, ga_s, gc_s, x_s, wa, wc, wo, g_ffn, tm):
    m = x.shape[0]
    ms = x_s.shape[0]
    row = lambda i: (i, 0)
    return pl.pallas_call(
        _merge_kernel,
        grid=(m // tm,),
        in_specs=[pl.BlockSpec((tm, ATTN_WIDTH), row), pl.BlockSpec((tm, CONV_DIM), row),
                  pl.BlockSpec((tm, D_MODEL), row), pl.BlockSpec((tm, D_MODEL), row),
                  pl.BlockSpec((tm, D_MODEL), row),
                  _whole((ms, ATTN_WIDTH)), _whole((ms, CONV_DIM)), _whole((ms, D_MODEL)), _whole((ms, D_MODEL)),
                  _whole((ms, D_MODEL)),
                  _const_spec((ATTN_WIDTH, D_MODEL)), _const_spec((CONV_DIM, D_MODEL)),
                  _const_spec((D_MODEL, D_MODEL)), _const_spec((1, D_MODEL))],
        out_specs=[pl.BlockSpec((tm, D_MODEL), row), pl.BlockSpec((tm, D_MODEL), row),
                   _whole((ms, D_MODEL)), _whole((ms, D_MODEL))],
        out_shape=[jax.ShapeDtypeStruct((m, D_MODEL), F32), jax.ShapeDtypeStruct((m, D_MODEL), BF16),
                   jax.ShapeDtypeStruct((ms, D_MODEL), F32), jax.ShapeDtypeStruct((ms, D_MODEL), BF16)],
        compiler_params=_cparams("arbitrary"),
        name="merge_out",
    )(o, c, ga, gc, x, o_s, c_s, ga_s, gc_s, x_s, wa, wc, wo, g_ffn)


FF_BLOCK = 512


def _ffn_kernel(h_ref, x1_ref, hs_ref, x1s_ref, wg_ref, wu_ref, wd_ref, gf_ref, y_ref, ys_ref, acc_scr, accs_scr):
    i = pl.program_id(0)
    j = pl.program_id(1)
    tm = h_ref.shape[0]
    last = pl.num_programs(1) - 1

    @pl.when(j == 0)
    def _():
        acc_scr[...] = jnp.zeros(acc_scr.shape, F32)
        accs_scr[...] = jnp.zeros(accs_scr.shape, F32)

    def run(with_sample):
        h = _rows(with_sample, h_ref, hs_ref)
        g = jnp.dot(h, wg_ref[...], preferred_element_type=F32)
        u = jnp.dot(h, wu_ref[...], preferred_element_type=F32)
        act = (g * jax.nn.sigmoid(g) * u).astype(BF16)
        part = jnp.dot(act, wd_ref[...], preferred_element_type=F32)
        acc_scr[...] += part[:tm]
        if with_sample:
            accs_scr[...] += part[tm:]

    _on_first_tile_too(i, run)

    @pl.when(j == last)
    def _():
        y_ref[...] = _rms(x1_ref[...] + acc_scr[...], gf_ref[...])

    @pl.when((j == last) & (i == 0))
    def _():
        ys_ref[...] = _rms(x1s_ref[...] + accs_scr[...], gf_ref[...])


def _ffn(h2, x1, h2_s, x1_s, w_gu, w_down, g_final, tm):
    m = h2.shape[0]
    ms = h2_s.shape[0]
    nj = D_FF // FF_BLOCK
    row = lambda i, j: (i, 0)
    whole = lambda i, j: (0, 0)
    return pl.pallas_call(
        _ffn_kernel,
        grid=(m // tm, nj),
        in_specs=[pl.BlockSpec((tm, D_MODEL), row), pl.BlockSpec((tm, D_MODEL), row),
                  pl.BlockSpec((ms, D_MODEL), whole), pl.BlockSpec((ms, D_MODEL), whole),
                  pl.BlockSpec((D_MODEL, FF_BLOCK), lambda i, j: (0, j)),
                  pl.BlockSpec((D_MODEL, FF_BLOCK), lambda i, j: (0, nj + j)),
                  pl.BlockSpec((FF_BLOCK, D_MODEL), lambda i, j: (j, 0)),
                  pl.BlockSpec((1, D_MODEL), whole)],
        out_specs=[pl.BlockSpec((tm, D_MODEL), row), pl.BlockSpec((ms, D_MODEL), whole)],
        out_shape=[jax.ShapeDtypeStruct((m, D_MODEL), F32), jax.ShapeDtypeStruct((ms, D_MODEL), F32)],
        scratch_shapes=[pltpu.VMEM((tm, D_MODEL), F32), pltpu.VMEM((ms, D_MODEL), F32)],
        compiler_params=_cparams("arbitrary", "arbitrary"),
        name="ffn",
    )(h2, x1, h2_s, x1_s, w_gu, w_gu, w_down, g_final)


def _rope_tables(pos):
    half = QK_DIM // 2
    inv = ROPE_THETA ** (-jnp.arange(half, dtype=F32) / half)
    ang = pos.astype(F32)[:, None] * inv[None, :]
    cos = jnp.tile(jnp.cos(ang), (1, LANES // half))
    sin = jnp.tile(jnp.sin(ang), (1, LANES // half))
    first_half = (jnp.arange(LANES) & half) == 0
    return cos, jnp.where(first_half[None, :], -sin, sin)


def kernel(x_prompt, x_sample, cache_k, cache_v, state_conv, page_table, w_in, w_conv, lambda_q1, lambda_k1, lambda_q2, lambda_k2, g_subln, w_branch_attn, w_branch_conv, w_out, g_mix, g_ffn, w_gate_up, w_down, g_final):
    bp, sp, _ = x_prompt.shape
    bd, sd, _ = x_sample.shape
    assert sd == 1 and w_in.shape[0] == 1
    n_pool = cache_k.shape[1]
    past = page_table.shape[1] * PAGE_SIZE
    lam_init = 0.8 - 0.6 * math.exp(-0.3 * 0)

    w_qkv = w_cv = w_gt = w_in[0].astype(BF16)
    g_mix2 = g_mix[0].reshape(1, D_MODEL)
    g_ffn2 = g_ffn[0].reshape(1, D_MODEL)
    g_fin2 = g_final.reshape(1, D_MODEL)
    g_sub2 = g_subln[0].reshape(1, HEAD_V)
    lam_vecs = jnp.concatenate([lambda_q1, lambda_k1, lambda_q2, lambda_k2, jnp.zeros((4, QK_DIM), F32)], axis=0)
    wcv3 = w_conv[0]

    mp = bp * sp
    xp = x_prompt.reshape(mp, D_MODEL)
    xs = x_sample.reshape(bd, D_MODEL)
    cos_p, sin_p = _rope_tables(jnp.arange(sp, dtype=jnp.int32))
    cos_s, sin_s = _rope_tables(jnp.full((bd,), past, dtype=jnp.int32))
    tm = 512
    hp, qq, k_p, kb, v_p, vb, hs, qq_s, k_s, v_s = _qkv(xp, xs, g_mix2, w_qkv, cos_p, sin_p, cos_s, sin_s, tm)
    c_p, st_p, c_s, st_s, w_gu = _conv(hp, hs, state_conv[0].reshape(bd, 2 * CONV_DIM), w_cv, wcv3, tm, sp,
                                       side=(w_gate_up[0],))
    ga_p, gc_p, ga_s, gc_s, w_dn, wo, wa, wc = _gates(
        hp, hs, w_gt, tm, side=(w_down[0], w_out[0], w_branch_attn[0], w_branch_conv[0]))
    qd = jnp.transpose(qq_s.reshape(2, bd, ATTN_HEADS, HEAD_V), (1, 0, 2, 3)).reshape(bd, ROWS, HEAD_V)
    k_new = jnp.tile(k_s.reshape(bd, ATTN_HEADS, HEAD_V), (1, 2, 1))
    v_new = jnp.tile(v_s.reshape(bd, ATTN_HEADS, HEAD_V), (1, 2, 1))

    o_p, o_s = _attn(qq, kb, vb, page_table, qd, k_new, v_new,
                     cache_k.reshape(n_pool, PAGE_ROWS, HEAD_V), cache_v.reshape(n_pool, PAGE_ROWS, HEAD_V),
                     lam_vecs, g_sub2, lam_init, bp, sp, 1024, 512)
    x1_p, h2_p, x1_s, h2_s = _merge(o_p, c_p, ga_p, gc_p, xp, o_s.reshape(bd, ATTN_WIDTH), c_s, ga_s, gc_s, xs,
                                    wa, wc, wo, g_ffn2, 256)
    y_p, y_s = _ffn(h2_p, x1_p, h2_s, x1_s, w_gu, w_dn, g_fin2, tm)

    return (y_p.reshape(bp, sp, D_MODEL),
            y_s.reshape(bd, sd, D_MODEL),
            k_p.reshape(1, bp, sp, ATTN_HEADS, HEAD_V),
            v_p.reshape(1, bp, sp, ATTN_HEADS, HEAD_V),
            st_p[:, HALO - 2:, :].reshape(1, bp, CONV_WIDTH - 1, CONV_DIM),
            k_s.reshape(1, bd, sd, ATTN_HEADS, HEAD_V),
            v_s.reshape(1, bd, sd, ATTN_HEADS, HEAD_V),
            st_s.reshape(1, bd, CONV_WIDTH - 1, CONV_DIM))
```

```python
import functools
import math

import jax
import jax.numpy as jnp
from jax import lax
from jax.experimental import pallas as pl
from jax.experimental.pallas import tpu as pltpu

F32 = jnp.float32
BF16 = jnp.bfloat16

D_MODEL = 2048
ATTN_HEADS = 8
QK_DIM = 64
HEAD_V = 128
ATTN_WIDTH = ATTN_HEADS * HEAD_V
CONV_DIM = D_MODEL // 2
CONV_WIDTH = 3
D_FF = 5632
ROPE_THETA = 10000.0
PAGE_SIZE = 128
EPS = 1e-5
SCALE = QK_DIM ** -0.5
Q_SCALE = SCALE * math.log2(math.e)
NEG_INF = -1e30
LANES = 128

VMEM_LIMIT = 56 * 1024 * 1024


def _cparams(*sem):
    return pltpu.CompilerParams(dimension_semantics=sem, vmem_limit_bytes=VMEM_LIMIT)


def _const_spec(shape):
    nd = len(shape)
    return pl.BlockSpec(shape, lambda *_: (0,) * nd, pipeline_mode=pl.Buffered(1))


QKV_BLOCK = (3 * ATTN_WIDTH, 0)
CONV_BLOCK = (3 * CONV_DIM, 1)
GATE_A_BLOCK = (D_MODEL, 3)
GATE_C_BLOCK = (D_MODEL, 4)
assert QKV_BLOCK[0] == CONV_BLOCK[0] and GATE_A_BLOCK[1] * D_MODEL == QKV_BLOCK[0] + CONV_BLOCK[0]


def _rms(x, g):
    return x * lax.rsqrt(jnp.mean(x * x, axis=-1, keepdims=True) + EPS) * g


def _rope(z, cos, sin_signed, first_half):
    partner = jnp.where(first_half, pltpu.roll(z, 96, 1), pltpu.roll(z, 32, 1))
    return z * cos + partner * sin_signed


def _on_first_tile_too(i, run):
    pl.when(i == 0)(functools.partial(run, True))
    pl.when(i != 0)(functools.partial(run, False))


def _rows(with_sample, tile_ref, sample_ref):
    return jnp.concatenate([tile_ref[...], sample_ref[...]], axis=0) if with_sample else tile_ref[...]


def _whole(shape):
    nd = len(shape)
    return pl.BlockSpec(shape, lambda *_: (0,) * nd)


def _qkv_kernel(n_side, x_ref, xs_ref, g_ref, w_ref, cos_ref, sin_ref, coss_ref, sins_ref, *rest):
    side_in, side_out = rest[:n_side], rest[n_side + 10:]
    h_ref, qq_ref, k_ref, kb_ref, v_ref, vb_ref, hs_ref, qqs_ref, ks_ref, vs_ref = rest[n_side:n_side + 10]
    _side_cast(side_in, side_out)
    tm = x_ref.shape[0]

    def run(with_sample):
        h = _rms(_rows(with_sample, x_ref, xs_ref), g_ref[...]).astype(BF16)
        z = jnp.dot(h, w_ref[...], preferred_element_type=F32)
        cos = _rows(with_sample, cos_ref, coss_ref)
        sin = _rows(with_sample, sin_ref, sins_ref)
        lane = lax.broadcasted_iota(jnp.int32, cos.shape, 1)
        first_half = (lane & 32) == 0
        map1 = lane < QK_DIM
        h_ref[...] = h[:tm]
        if with_sample:
            hs_ref[...] = h[tm:]
        for hh in range(ATTN_HEADS):
            cols = slice(hh * HEAD_V, (hh + 1) * HEAD_V)
            q = _rope(z[:, cols], cos, sin, first_half) * Q_SCALE
            q1 = jnp.where(map1, q, 0.0).astype(BF16)
            q2 = jnp.where(map1, 0.0, q).astype(BF16)
            k = _rope(z[:, ATTN_WIDTH + hh * HEAD_V:ATTN_WIDTH + (hh + 1) * HEAD_V], cos, sin, first_half)
            qq_ref[0, :, cols] = q1[:tm]
            qq_ref[1, :, cols] = q2[:tm]
            k_ref[:, cols] = k[:tm]
            kb_ref[:, cols] = k[:tm].astype(BF16)
            if with_sample:
                qqs_ref[0, :, cols] = q1[tm:]
                qqs_ref[1, :, cols] = q2[tm:]
                ks_ref[:, cols] = k[tm:]
        v = z[:, 2 * ATTN_WIDTH:]
        v_ref[...] = v[:tm]
        vb_ref[...] = v[:tm].astype(BF16)
        if with_sample:
            vs_ref[...] = v[tm:]

    _on_first_tile_too(pl.program_id(0), run)


def _qkv(x, xs, g, w, cos, sin_signed, cos_s, sin_s, tm, side=()):
    m = x.shape[0]
    ms = xs.shape[0]
    nt = cos.shape[0] // tm
    row = lambda i: (i, 0)
    side_in, side_out, side_shapes = _side_specs(side, m // tm)
    return pl.pallas_call(
        functools.partial(_qkv_kernel, len(side)),
        grid=(m // tm,),
        in_specs=[pl.BlockSpec((tm, D_MODEL), row), _whole((ms, D_MODEL)),
                  _const_spec((1, D_MODEL)),
                  _const_spec((D_MODEL, 3 * ATTN_WIDTH)),
                  pl.BlockSpec((tm, LANES), lambda i: (i % nt, 0)),
                  pl.BlockSpec((tm, LANES), lambda i: (i % nt, 0)),
                  _whole((ms, LANES)), _whole((ms, LANES))] + side_in,
        out_specs=[pl.BlockSpec((tm, D_MODEL), row),
                   pl.BlockSpec((2, tm, ATTN_WIDTH), lambda i: (0, i, 0)),
                   pl.BlockSpec((tm, ATTN_WIDTH), row), pl.BlockSpec((tm, ATTN_WIDTH), row),
                   pl.BlockSpec((tm, ATTN_WIDTH), row), pl.BlockSpec((tm, ATTN_WIDTH), row),
                   _whole((ms, D_MODEL)), _whole((2, ms, ATTN_WIDTH)),
                   _whole((ms, ATTN_WIDTH)), _whole((ms, ATTN_WIDTH))] + side_out,
        out_shape=[jax.ShapeDtypeStruct((m, D_MODEL), BF16),
                   jax.ShapeDtypeStruct((2, m, ATTN_WIDTH), BF16),
                   jax.ShapeDtypeStruct((m, ATTN_WIDTH), F32), jax.ShapeDtypeStruct((m, ATTN_WIDTH), BF16),
                   jax.ShapeDtypeStruct((m, ATTN_WIDTH), F32), jax.ShapeDtypeStruct((m, ATTN_WIDTH), BF16),
                   jax.ShapeDtypeStruct((ms, D_MODEL), BF16), jax.ShapeDtypeStruct((2, ms, ATTN_WIDTH), BF16),
                   jax.ShapeDtypeStruct((ms, ATTN_WIDTH), F32),
                   jax.ShapeDtypeStruct((ms, ATTN_WIDTH), F32)] + side_shapes,
        compiler_params=_cparams("arbitrary"),
        name="qkv_rope",
    )(x, xs, g, w, cos, sin_signed, cos_s, sin_s, *[a for a, _, _ in side])


HALO = 8
CONV_CHUNK = 256


def _side_specs(side, steps):
    in_specs, out_specs, shapes = [], [], []
    for a, width, block in side:
        rows = a.shape[0] // steps
        assert rows * steps == a.shape[0] and rows % 16 == 0 and (block + 1) * width <= a.shape[1]
        in_specs.append(pl.BlockSpec((rows, width), lambda i, block=block: (i, block)))
        out_specs.append(pl.BlockSpec((rows, width), lambda i: (i, 0)))
        shapes.append(jax.ShapeDtypeStruct((a.shape[0], width), BF16))
    return in_specs, out_specs, shapes


def _whole_cols(a):
    return (a, a.shape[1], 0)


def _side_cast(src_refs, dst_refs):
    for src, dst in zip(src_refs, dst_refs):
        dst[...] = src[...].astype(BF16)


def _conv_kernel(tiles_per_seq, n_side, h_ref, hs_ref, sst_ref, w_ref, wc_ref, *rest):
    side_in, side_out, u_scr = rest[:n_side], rest[n_side + 4:-1], rest[-1]
    c_ref, st_ref, cs_ref, nst_ref = rest[n_side:n_side + 4]
    _side_cast(side_in, side_out)
    i = pl.program_id(0)
    tm = h_ref.shape[0]

    @pl.when(i % tiles_per_seq == 0)
    def _():
        u_scr[0:HALO, :] = jnp.zeros((HALO, CONV_DIM), F32)

    @pl.when(i % tiles_per_seq != 0)
    def _():
        u_scr[0:HALO, :] = u_scr[tm:tm + HALO, :]

    def run(with_sample):
        h = _rows(with_sample, h_ref, hs_ref)
        wc = wc_ref[...]
        for c0 in range(0, CONV_DIM, CONV_CHUNK):
            ch = slice(c0, c0 + CONV_CHUNK)
            proj = lambda part: jnp.dot(h, w_ref[:, part * CONV_DIM + c0:part * CONV_DIM + c0 + CONV_CHUNK],
                                        preferred_element_type=F32)
            gate = proj(0)
            u_all = proj(1) * proj(2)
            u = u_all[:tm]
            u_scr[HALO:HALO + tm, ch] = u
            y = (wc[0:1, ch] * u_scr[HALO - 2:HALO - 2 + tm, ch]
                 + wc[1:2, ch] * u_scr[HALO - 1:HALO - 1 + tm, ch]
                 + wc[2:3, ch] * u)
            c_ref[:, ch] = (gate[:tm] * y).astype(BF16)
            st_ref[:, ch] = u[tm - HALO:, :]
            if with_sample:
                us = u_all[tm:]
                s0 = sst_ref[:, ch]
                s1 = sst_ref[:, CONV_DIM + c0:CONV_DIM + c0 + CONV_CHUNK]
                ys = wc[0:1, ch] * s0 + wc[1:2, ch] * s1 + wc[2:3, ch] * us
                cs_ref[:, ch] = (gate[tm:] * ys).astype(BF16)
                nst_ref[:, ch] = s1
                nst_ref[:, CONV_DIM + c0:CONV_DIM + c0 + CONV_CHUNK] = us

    _on_first_tile_too(i, run)


def _conv(h, hs, state2d, w, w_conv, tm, seq, side=()):
    m = h.shape[0]
    ms = hs.shape[0]
    tiles_per_seq = seq // tm
    side_in, side_out, side_shapes = _side_specs(side, m // tm)
    return pl.pallas_call(
        functools.partial(_conv_kernel, tiles_per_seq, len(side)),
        grid=(m // tm,),
        in_specs=[pl.BlockSpec((tm, D_MODEL), lambda i: (i, 0)), _whole((ms, D_MODEL)), _whole((ms, 2 * CONV_DIM)),
                  _const_spec((D_MODEL, 3 * CONV_DIM)),
                  _const_spec((CONV_WIDTH, CONV_DIM))] + side_in,
        out_specs=[pl.BlockSpec((tm, CONV_DIM), lambda i: (i, 0)),
                   pl.BlockSpec((None, HALO, CONV_DIM), lambda i: (i // tiles_per_seq, 0, 0)),
                   _whole((ms, CONV_DIM)), _whole((ms, 2 * CONV_DIM))] + side_out,
        out_shape=[jax.ShapeDtypeStruct((m, CONV_DIM), BF16),
                   jax.ShapeDtypeStruct((m // seq, HALO, CONV_DIM), F32),
                   jax.ShapeDtypeStruct((ms, CONV_DIM), BF16),
                   jax.ShapeDtypeStruct((ms, 2 * CONV_DIM), F32)] + side_shapes,
        scratch_shapes=[pltpu.VMEM((tm + HALO, CONV_DIM), F32)],
        compiler_params=_cparams("arbitrary"),
        name="conv",
    )(h, hs, state2d, w, w_conv, *[a for a, _, _ in side])


def _gate_kernel(n_side, h_ref, hs_ref, wa_ref, wc_ref, *rest):
    side_in, side_out = rest[:n_side], rest[n_side + 4:]
    ga_ref, gc_ref, gas_ref, gcs_ref = rest[n_side:n_side + 4]
    _side_cast(side_in, side_out)
    tm = h_ref.shape[0]

    def run(with_sample):
        h = _rows(with_sample, h_ref, hs_ref)
        for w_ref, tile_out, sample_out in ((wa_ref, ga_ref, gas_ref), (wc_ref, gc_ref, gcs_ref)):
            gate = jax.nn.sigmoid(jnp.dot(h, w_ref[...], preferred_element_type=F32)).astype(BF16)
            tile_out[...] = gate[:tm]
            if with_sample:
                sample_out[...] = gate[tm:]

    _on_first_tile_too(pl.program_id(0), run)


def _gates(h, hs, w_a, w_c, tm, side=()):
    m = h.shape[0]
    ms = hs.shape[0]
    row = lambda i: (i, 0)
    side_in, side_out, side_shapes = _side_specs(side, m // tm)
    return pl.pallas_call(
        functools.partial(_gate_kernel, len(side)),
        grid=(m // tm,),
        in_specs=[pl.BlockSpec((tm, D_MODEL), row), _whole((ms, D_MODEL)),
                  _const_spec((D_MODEL, D_MODEL)), _const_spec((D_MODEL, D_MODEL))] + side_in,
        out_specs=[pl.BlockSpec((tm, D_MODEL), row), pl.BlockSpec((tm, D_MODEL), row),
                   _whole((ms, D_MODEL)), _whole((ms, D_MODEL))] + side_out,
        out_shape=[jax.ShapeDtypeStruct((m, D_MODEL), BF16), jax.ShapeDtypeStruct((m, D_MODEL), BF16),
                   jax.ShapeDtypeStruct((ms, D_MODEL), BF16), jax.ShapeDtypeStruct((ms, D_MODEL), BF16)] + side_shapes,
        compiler_params=_cparams("arbitrary"),
        name="gates",
    )(h, hs, w_a, w_c, *[a for a, _, _ in side])


def _lam(lam_ref, lam_init):
    lv = lam_ref[...]
    a = jnp.sum(lv[0:1, :] * lv[1:2, :], axis=-1, keepdims=True)
    b = jnp.sum(lv[2:3, :] * lv[3:4, :], axis=-1, keepdims=True)
    return jnp.exp(a) - jnp.exp(b) + lam_init


def _subln(o, g, lam_init):
    return o * lax.rsqrt(jnp.mean(o * o, axis=-1, keepdims=True) + EPS) * g * (1.0 - lam_init)


ROW_BLOCK = 128


def _attn_tile(diag, qq_ref, k_ref, v_ref, m_scr, acc_scr):
    tq = qq_ref.shape[1]
    tk = k_ref.shape[0]
    k = k_ref[...]
    v_ext = jnp.concatenate([v_ref[...], jnp.ones((tk, HEAD_V), BF16)], axis=1)
    for mp in range(2):
        for r0 in range(0, tq, ROW_BLOCK):
            c0 = 0 if diag is None else diag * tk
            if r0 + ROW_BLOCK - 1 < c0:
                continue
            rows = pl.ds(mp * tq + r0, ROW_BLOCK)
            s = lax.dot_general(qq_ref[mp, r0:r0 + ROW_BLOCK, :], k, (((1,), (1,)), ((), ())),
                                preferred_element_type=F32)
            if diag is not None and r0 < c0 + tk - 1:
                qpos = r0 + lax.broadcasted_iota(jnp.int32, s.shape, 0)
                kpos = c0 + lax.broadcasted_iota(jnp.int32, s.shape, 1)
                s = jnp.where(kpos <= qpos, s, NEG_INF)
            m_prev = m_scr[rows, :]
            m_new = jnp.maximum(m_prev, jnp.max(s, axis=1, keepdims=True))
            p = jnp.exp2(s - jnp.tile(m_new, (1, tk // LANES)))
            alpha = jnp.exp2(m_prev - m_new)
            pv = jnp.dot(p.astype(BF16), v_ext, preferred_element_type=F32)
            acc_scr[rows, :] = jnp.tile(alpha, (1, 2)) * acc_scr[rows, :] + pv
            m_scr[rows, :] = m_new


def _attn_prompt_step(lam_init, i, j, qq_ref, k_ref, v_ref, lam_ref, g_ref, o_ref, m_scr, acc_scr):
    tq = qq_ref.shape[1]
    ratio = tq // k_ref.shape[0]

    @pl.when(j == 0)
    def _():
        m_scr[...] = jnp.full(m_scr.shape, NEG_INF, F32)
        acc_scr[...] = jnp.zeros(acc_scr.shape, F32)

    @pl.when(j < ratio * i)
    def _():
        _attn_tile(None, qq_ref, k_ref, v_ref, m_scr, acc_scr)

    for d in range(ratio):
        @pl.when(j == ratio * i + d)
        def _(d=d):
            _attn_tile(d, qq_ref, k_ref, v_ref, m_scr, acc_scr)

    @pl.when(j == ratio * i + ratio - 1)
    def _():
        acc = acc_scr[...]
        o12 = acc[:, :HEAD_V] / acc[:, HEAD_V:]
        o = o12[:tq] - _lam(lam_ref, lam_init) * o12[tq:]
        o_ref[...] = _subln(o, g_ref[...], lam_init).astype(BF16)


ROWS = 2 * ATTN_HEADS
PAGE_ROWS = PAGE_SIZE * ATTN_HEADS


def _attn_sample_step(lam_init, first, last, tail, qd_ref, kn_ref, vn_ref, hb_ref, lam_ref, g_ref, k_refs, v_refs,
                      o_ref, m_scr, l_scr, acc_scr):
    n = len(k_refs)
    qd = qd_ref[...]
    slot_bias = jnp.where(last, NEG_INF, 0.0)

    @pl.when(first)
    def _():
        m_scr[...] = jnp.sum(qd.astype(F32) * kn_ref[...], axis=-1, keepdims=True)
        l_scr[...] = jnp.ones(l_scr.shape, F32)
        acc_scr[...] = vn_ref[...]

    head_bias = hb_ref[...]
    pieces = [lax.dot_general(qd, k_refs[p][...].astype(BF16), (((1,), (1,)), ((), ())),
                              preferred_element_type=F32) + head_bias for p in range(n)]
    pieces = [sp if p < tail else sp + slot_bias for p, sp in enumerate(pieces)]
    s = jnp.concatenate(pieces, axis=1)
    m_prev = m_scr[...]
    m_new = jnp.maximum(m_prev, jnp.max(s, axis=-1, keepdims=True))
    alpha = jnp.exp2(m_prev - m_new)
    p_all = jnp.exp2(s - m_new)
    l_scr[...] = alpha * l_scr[...] + jnp.sum(p_all, axis=-1, keepdims=True)
    pb = p_all.astype(BF16)
    pv = jnp.dot(pb[:, 0:PAGE_ROWS], v_refs[0][...].astype(BF16), preferred_element_type=F32)
    for p in range(1, n):
        pv += jnp.dot(pb[:, p * PAGE_ROWS:(p + 1) * PAGE_ROWS], v_refs[p][...].astype(BF16),
                      preferred_element_type=F32)
    acc_scr[...] = alpha * acc_scr[...] + pv
    m_scr[...] = m_new

    @pl.when(last)
    def _():
        o12 = acc_scr[...] / l_scr[...]
        o = o12[:ATTN_HEADS] - _lam(lam_ref, lam_init) * o12[ATTN_HEADS:]
        o_ref[...] = _subln(o, g_ref[...], lam_init).astype(BF16)


def _chunk_copies(start, pg_ref, sq_ref, ck_hbm, cv_hbm, kbuf, vbuf, sem, n, steps_per_seq, tail, chunk, slot):
    def run(lo, hi):
        for p in range(lo, hi):
            page = pg_ref[chunk * n + p]
            for src, dst, kind in ((ck_hbm, kbuf, 0), (cv_hbm, vbuf, 1)):
                copy = pltpu.make_async_copy(src.at[page], dst.at[slot, p], sem.at[kind, slot])
                if start:
                    copy.start()
                else:
                    copy.wait()

    run(0, tail)
    if tail < n:
        is_last = chunk - sq_ref[chunk] * steps_per_seq == steps_per_seq - 1
        pl.when(jnp.logical_not(is_last))(lambda: run(tail, n))


def _attn_kernel(lam_init, n, steps_per_seq, tail, it_ref, jt_ref, pg_ref, sq_ref,
                 qq_ref, k_ref, v_ref, lam_ref, g_ref, qd_ref, kn_ref, vn_ref, hb_ref, ck_hbm, cv_hbm,
                 o_ref, os_ref, m_scr, acc_scr, dm_scr, dl_scr, dacc_scr, kbuf, vbuf, sem):
    step = pl.program_id(2)
    steps = pl.num_programs(0) * pl.num_programs(1) * pl.num_programs(2)
    lin = (pl.program_id(0) * pl.num_programs(1) + pl.program_id(1)) * pl.num_programs(2) + step
    slot = lin % 2
    copies = functools.partial(_chunk_copies, pg_ref=pg_ref, sq_ref=sq_ref, ck_hbm=ck_hbm, cv_hbm=cv_hbm,
                               kbuf=kbuf, vbuf=vbuf, sem=sem, n=n, steps_per_seq=steps_per_seq, tail=tail)

    @pl.when(lin == 0)
    def _():
        copies(True, chunk=0, slot=0)

    @pl.when(lin + 1 < steps)
    def _():
        copies(True, chunk=lin + 1, slot=1 - slot)

    _attn_prompt_step(lam_init, it_ref[step], jt_ref[step], qq_ref, k_ref, v_ref, lam_ref, g_ref, o_ref,
                      m_scr, acc_scr)

    copies(False, chunk=lin, slot=slot)
    t = lin - sq_ref[lin] * steps_per_seq
    _attn_sample_step(lam_init, t == 0, t == steps_per_seq - 1, tail, qd_ref, kn_ref, vn_ref, hb_ref, lam_ref, g_ref,
                      [kbuf.at[slot, p] for p in range(n)], [vbuf.at[slot, p] for p in range(n)],
                      os_ref, dm_scr, dl_scr, dacc_scr)


def _attn(qq, kb, vb, page_table, qd, k_new, v_new, ck, cv, lam_vecs, g_subln, lam_init, batch, seq, tq, tk):
    m = kb.shape[0]
    nq = seq // tq
    nk = seq // tk
    ratio = tq // tk
    pairs = [(i, j) for i in range(nq) for j in range(ratio * (i + 1))]
    i_tab = jnp.array([p[0] for p in pairs], jnp.int32)
    j_tab = jnp.array([p[1] for p in pairs], jnp.int32)
    npairs = len(pairs)

    bd, n_pages = page_table.shape
    steps = batch * ATTN_HEADS * npairs
    assert steps % bd == 0
    steps_per_seq = steps // bd
    n = -(-n_pages // steps_per_seq)
    tail = n_pages - (steps_per_seq - 1) * n
    assert 0 < tail <= n
    seq_of_step = [t // steps_per_seq for t in range(steps)]
    slot_page = [[min((t % steps_per_seq) * n + p, n_pages - 1) for p in range(n)] for t in range(steps)]
    pages = page_table[jnp.array(seq_of_step, jnp.int32)[:, None], jnp.array(slot_page, jnp.int32)].reshape(steps * n)
    seq_tab = jnp.array(seq_of_step, jnp.int32)
    head_bias = jnp.where(jnp.arange(PAGE_ROWS)[None, :] % ATTN_HEADS == jnp.arange(ROWS)[:, None] % ATTN_HEADS,
                          0.0, NEG_INF).astype(F32)

    def lin(b, h, s):
        return (b * ATTN_HEADS + h) * npairs + s

    seq_row = lambda b, h, s, it, jt, pg, sq: (sq[lin(b, h, s)], 0, 0)
    fixed = lambda b, h, s, it, jt, pg, sq: (0, 0)
    grid_spec = pltpu.PrefetchScalarGridSpec(
        num_scalar_prefetch=4,
        grid=(batch, ATTN_HEADS, npairs),
        in_specs=[pl.BlockSpec((2, tq, HEAD_V), lambda b, h, s, it, jt, pg, sq: (0, b * nq + it[s], h)),
                  pl.BlockSpec((tk, HEAD_V), lambda b, h, s, it, jt, pg, sq: (b * nk + jt[s], h)),
                  pl.BlockSpec((tk, HEAD_V), lambda b, h, s, it, jt, pg, sq: (b * nk + jt[s], h)),
                  pl.BlockSpec((8, QK_DIM), fixed),
                  pl.BlockSpec((1, HEAD_V), fixed),
                  pl.BlockSpec((None, ROWS, HEAD_V), seq_row),
                  pl.BlockSpec((None, ROWS, HEAD_V), seq_row),
                  pl.BlockSpec((None, ROWS, HEAD_V), seq_row),
                  pl.BlockSpec((ROWS, PAGE_ROWS), fixed),
                  pl.BlockSpec(memory_space=pl.ANY), pl.BlockSpec(memory_space=pl.ANY)],
        out_specs=[pl.BlockSpec((tq, HEAD_V), lambda b, h, s, it, jt, pg, sq: (b * nq + it[s], h)),
                   pl.BlockSpec((None, ATTN_HEADS, HEAD_V), seq_row)],
        scratch_shapes=[pltpu.VMEM((2 * tq, LANES), F32), pltpu.VMEM((2 * tq, 2 * HEAD_V), F32),
                        pltpu.VMEM((ROWS, 1), F32), pltpu.VMEM((ROWS, 1), F32), pltpu.VMEM((ROWS, HEAD_V), F32),
                        pltpu.VMEM((2, n, PAGE_ROWS, HEAD_V), F32), pltpu.VMEM((2, n, PAGE_ROWS, HEAD_V), F32),
                        pltpu.SemaphoreType.DMA((2, 2))],
    )
    return pl.pallas_call(
        functools.partial(_attn_kernel, lam_init, n, steps_per_seq, tail),
        grid_spec=grid_spec,
        out_shape=[jax.ShapeDtypeStruct((m, ATTN_WIDTH), BF16),
                   jax.ShapeDtypeStruct((bd, ATTN_HEADS, HEAD_V), BF16)],
        compiler_params=_cparams("arbitrary", "arbitrary", "arbitrary"),
        name="attn",
    )(i_tab, j_tab, pages, seq_tab, qq, kb, vb, lam_vecs, g_subln, qd, k_new, v_new, head_bias, ck, cv)


def _merge_kernel(o_ref, c_ref, ga_ref, gc_ref, x_ref, os_ref, cs_ref, gas_ref, gcs_ref, xs_ref,
                  wa_ref, wc_ref, wo_ref, g_ref, x1_ref, h2_ref, x1s_ref, h2s_ref):
    tm = x_ref.shape[0]

    def run(with_sample):
        rows = functools.partial(_rows, with_sample)
        a_out = jnp.dot(rows(o_ref, os_ref), wa_ref[...], preferred_element_type=F32)
        c_out = jnp.dot(rows(c_ref, cs_ref), wc_ref[...], preferred_element_type=F32)
        merged = rows(ga_ref, gas_ref).astype(F32) * a_out + rows(gc_ref, gcs_ref).astype(F32) * c_out
        x1 = rows(x_ref, xs_ref) + jnp.dot(merged.astype(BF16), wo_ref[...], preferred_element_type=F32)
        h2 = _rms(x1, g_ref[...]).astype(BF16)
        x1_ref[...] = x1[:tm]
        h2_ref[...] = h2[:tm]
        if with_sample:
            x1s_ref[...] = x1[tm:]
            h2s_ref[...] = h2[tm:]

    _on_first_tile_too(pl.program_id(0), run)


def _merge(o, c, ga, gc, x, o_s, c_s, ga_s, gc_s, x_s, wa, wc, wo, g_ffn, tm):
    m = x.shape[0]
    ms = x_s.shape[0]
    row = lambda i: (i, 0)
    return pl.pallas_call(
        _merge_kernel,
        grid=(m // tm,),
        in_specs=[pl.BlockSpec((tm, ATTN_WIDTH), row), pl.BlockSpec((tm, CONV_DIM), row),
                  pl.BlockSpec((tm, D_MODEL), row), pl.BlockSpec((tm, D_MODEL), row),
                  pl.BlockSpec((tm, D_MODEL), row),
                  _whole((ms, ATTN_WIDTH)), _whole((ms, CONV_DIM)), _whole((ms, D_MODEL)), _whole((ms, D_MODEL)),
                  _whole((ms, D_MODEL)),
                  _const_spec((ATTN_WIDTH, D_MODEL)), _const_spec((CONV_DIM, D_MODEL)),
                  _const_spec((D_MODEL, D_MODEL)), _const_spec((1, D_MODEL))],
        out_specs=[pl.BlockSpec((tm, D_MODEL), row), pl.BlockSpec((tm, D_MODEL), row),
                   _whole((ms, D_MODEL)), _whole((ms, D_MODEL))],
        out_shape=[jax.ShapeDtypeStruct((m, D_MODEL), F32), jax.ShapeDtypeStruct((m, D_MODEL), BF16),
                   jax.ShapeDtypeStruct((ms, D_MODEL), F32), jax.ShapeDtypeStruct((ms, D_MODEL), BF16)],
        compiler_params=_cparams("arbitrary"),
        name="merge_out",
    )(o, c, ga, gc, x, o_s, c_s, ga_s, gc_s, x_s, wa, wc, wo, g_ffn)


FF_BLOCK = 512


def _ffn_kernel(h_ref, x1_ref, hs_ref, x1s_ref, wg_ref, wu_ref, wd_ref, gf_ref, y_ref, ys_ref, acc_scr, accs_scr):
    i = pl.program_id(0)
    j = pl.program_id(1)
    tm = h_ref.shape[0]
    last = pl.num_programs(1) - 1

    @pl.when(j == 0)
    def _():
        acc_scr[...] = jnp.zeros(acc_scr.shape, F32)
        accs_scr[...] = jnp.zeros(accs_scr.shape, F32)

    def run(with_sample):
        h = _rows(with_sample, h_ref, hs_ref)
        g = jnp.dot(h, wg_ref[...], preferred_element_type=F32)
        u = jnp.dot(h, wu_ref[...], preferred_element_type=F32)
        act = (g * jax.nn.sigmoid(g) * u).astype(BF16)
        part = jnp.dot(act, wd_ref[...], preferred_element_type=F32)
        acc_scr[...] += part[:tm]
        if with_sample:
            accs_scr[...] += part[tm:]

    _on_first_tile_too(i, run)

    @pl.when(j == last)
    def _():
        y_ref[...] = _rms(x1_ref[...] + acc_scr[...], gf_ref[...])

    @pl.when((j == last) & (i == 0))
    def _():
        ys_ref[...] = _rms(x1s_ref[...] + accs_scr[...], gf_ref[...])


def _ffn(h2, x1, h2_s, x1_s, w_gu, w_down, g_final, tm):
    m = h2.shape[0]
    ms = h2_s.shape[0]
    nj = D_FF // FF_BLOCK
    row = lambda i, j: (i, 0)
    whole = lambda i, j: (0, 0)
    return pl.pallas_call(
        _ffn_kernel,
        grid=(m // tm, nj),
        in_specs=[pl.BlockSpec((tm, D_MODEL), row), pl.BlockSpec((tm, D_MODEL), row),
                  pl.BlockSpec((ms, D_MODEL), whole), pl.BlockSpec((ms, D_MODEL), whole),
                  pl.BlockSpec((D_MODEL, FF_BLOCK), lambda i, j: (0, j)),
                  pl.BlockSpec((D_MODEL, FF_BLOCK), lambda i, j: (0, nj + j)),
                  pl.BlockSpec((FF_BLOCK, D_MODEL), lambda i, j: (j, 0)),
                  pl.BlockSpec((1, D_MODEL), whole)],
        out_specs=[pl.BlockSpec((tm, D_MODEL), row), pl.BlockSpec((ms, D_MODEL), whole)],
        out_shape=[jax.ShapeDtypeStruct((m, D_MODEL), F32), jax.ShapeDtypeStruct((ms, D_MODEL), F32)],
        scratch_shapes=[pltpu.VMEM((tm, D_MODEL), F32), pltpu.VMEM((ms, D_MODEL), F32)],
        compiler_params=_cparams("arbitrary", "arbitrary"),
        name="ffn",
    )(h2, x1, h2_s, x1_s, w_gu, w_gu, w_down, g_final)


def _rope_tables(pos):
    half = QK_DIM // 2
    inv = ROPE_THETA ** (-jnp.arange(half, dtype=F32) / half)
    ang = pos.astype(F32)[:, None] * inv[None, :]
    cos = jnp.tile(jnp.cos(ang), (1, LANES // half))
    sin = jnp.tile(jnp.sin(ang), (1, LANES // half))
    first_half = (jnp.arange(LANES) & half) == 0
    return cos, jnp.where(first_half[None, :], -sin, sin)


def kernel(x_prompt, x_sample, cache_k, cache_v, state_conv, page_table, w_in, w_conv, lambda_q1, lambda_k1, lambda_q2, lambda_k2, g_subln, w_branch_attn, w_branch_conv, w_out, g_mix, g_ffn, w_gate_up, w_down, g_final):
    bp, sp, _ = x_prompt.shape
    bd, sd, _ = x_sample.shape
    assert sd == 1 and w_in.shape[0] == 1
    n_pool = cache_k.shape[1]
    past = page_table.shape[1] * PAGE_SIZE
    lam_init = 0.8 - 0.6 * math.exp(-0.3 * 0)

    w_in2 = w_in[0]
    w_qkv = w_in2[:, :QKV_BLOCK[0]].astype(BF16)
    g_mix2 = g_mix[0].reshape(1, D_MODEL)
    g_ffn2 = g_ffn[0].reshape(1, D_MODEL)
    g_fin2 = g_final.reshape(1, D_MODEL)
    g_sub2 = g_subln[0].reshape(1, HEAD_V)
    lam_vecs = jnp.concatenate([lambda_q1, lambda_k1, lambda_q2, lambda_k2, jnp.zeros((4, QK_DIM), F32)], axis=0)
    wcv3 = w_conv[0]

    mp = bp * sp
    xp = x_prompt.reshape(mp, D_MODEL)
    xs = x_sample.reshape(bd, D_MODEL)
    cos_p, sin_p = _rope_tables(jnp.arange(sp, dtype=jnp.int32))
    cos_s, sin_s = _rope_tables(jnp.full((bd,), past, dtype=jnp.int32))
    tm = 512
    hp, qq, k_p, kb, v_p, vb, hs, qq_s, k_s, v_s, w_cv = _qkv(
        xp, xs, g_mix2, w_qkv, cos_p, sin_p, cos_s, sin_s, tm, side=((w_in2,) + CONV_BLOCK,))
    c_p, st_p, c_s, st_s, w_ga, w_gc, w_gu = _conv(
        hp, hs, state_conv[0].reshape(bd, 2 * CONV_DIM), w_cv, wcv3, tm, sp,
        side=((w_in2,) + GATE_A_BLOCK, (w_in2,) + GATE_C_BLOCK, _whole_cols(w_gate_up[0])))
    ga_p, gc_p, ga_s, gc_s, w_dn, wo, wa, wc = _gates(
        hp, hs, w_ga, w_gc, tm, side=tuple(_whole_cols(a) for a in (w_down[0], w_out[0], w_branch_attn[0],
                                                                    w_branch_conv[0])))
    qd = jnp.transpose(qq_s.reshape(2, bd, ATTN_HEADS, HEAD_V), (1, 0, 2, 3)).reshape(bd, ROWS, HEAD_V)
    k_new = jnp.tile(k_s.reshape(bd, ATTN_HEADS, HEAD_V), (1, 2, 1))
    v_new = jnp.tile(v_s.reshape(bd, ATTN_HEADS, HEAD_V), (1, 2, 1))

    o_p, o_s = _attn(qq, kb, vb, page_table, qd, k_new, v_new,
                     cache_k.reshape(n_pool, PAGE_ROWS, HEAD_V), cache_v.reshape(n_pool, PAGE_ROWS, HEAD_V),
                     lam_vecs, g_sub2, lam_init, bp, sp, 1024, 512)
    x1_p, h2_p, x1_s, h2_s = _merge(o_p, c_p, ga_p, gc_p, xp, o_s.reshape(bd, ATTN_WIDTH), c_s, ga_s, gc_s, xs,
                                    wa, wc, wo, g_ffn2, 256)
    y_p, y_s = _ffn(h2_p, x1_p, h2_s, x1_s, w_gu, w_dn, g_fin2, tm)

    return (y_p.reshape(bp, sp, D_MODEL),
            y_s.reshape(bd, sd, D_MODEL),
            k_p.reshape(1, bp, sp, ATTN_HEADS, HEAD_V),
            v_p.reshape(1, bp, sp, ATTN_HEADS, HEAD_V),
            st_p[:, HALO - 2:, :].reshape(1, bp, CONV_WIDTH - 1, CONV_DIM),
            k_s.reshape(1, bd, sd, ATTN_HEADS, HEAD_V),
            v_s.reshape(1, bd, sd, ATTN_HEADS, HEAD_V),
            st_s.reshape(1, bd, CONV_WIDTH - 1, CONV_DIM))
```
